```python
import jax, jax.numpy as jnp
from jax import lax
import numpy as np

D_MODEL = 4096
BATCH = 4
SEQ = 4096
DEPTH = 1

CONV_W = 4
LRU_WIDTH = D_MODEL // 2
LRU_BLOCKS = 16
LRU_BLOCK = LRU_WIDTH // LRU_BLOCKS
LRU_C = 8.0
GDN_DK = 128
GDN_DV = 128
GDN_HEADS = (D_MODEL // 2) // GDN_DK
GDN_KEY = GDN_HEADS * GDN_DK
GDN_VAL = GDN_HEADS * GDN_DV
GDN_CONV_DIM = 2 * GDN_KEY + GDN_VAL
CHUNK = 64
D_FF = 4 * D_MODEL
EPS = 1e-6
IN_SIZES = (LRU_WIDTH, LRU_WIDTH, GDN_CONV_DIM, GDN_VAL, GDN_HEADS, GDN_HEADS, D_MODEL, D_MODEL)
IN_WIDTH = sum(IN_SIZES)

kernel_name = "hybrid_rglru_gdn_adaln_block"


def rms_norm(x, w):
    xf = x.astype(jnp.float32)
    y = xf * lax.rsqrt(jnp.mean(xf * xf, axis=-1, keepdims=True) + EPS)
    return (y * w.astype(jnp.float32)).astype(x.dtype)


def l2_normalize(x):
    xf = x.astype(jnp.float32)
    return xf * lax.rsqrt(jnp.sum(xf * xf, axis=-1, keepdims=True) + EPS)


def causal_depthwise_conv(x, w):
    C = x.shape[-1]
    return lax.conv_general_dilated(
        x, w[:, None, :].astype(x.dtype), window_strides=(1,), padding=[(CONV_W - 1, 0)],
        dimension_numbers=("NWC", "WIO", "NWC"), feature_group_count=C)


def rg_lru(x, w_a, b_a, w_i, b_i, lam):
    B, T, _ = x.shape
    xb = x.reshape(B, T, LRU_BLOCKS, LRU_BLOCK)
    r = jax.nn.sigmoid(jnp.einsum('btni,nij->btnj', xb, w_a) + b_a).reshape(B, T, LRU_WIDTH)
    i = jax.nn.sigmoid(jnp.einsum('btni,nij->btnj', xb, w_i) + b_i).reshape(B, T, LRU_WIDTH)
    log_a = -LRU_C * r.astype(jnp.float32) * jax.nn.softplus(-lam.astype(jnp.float32))
    a = jnp.exp(log_a)
    u = jnp.sqrt(-jnp.expm1(2.0 * log_a)) * (i * x).astype(jnp.float32)

    def combine(left, right):
        a1, b1 = left
        a2, b2 = right
        return a1 * a2, a2 * b1 + b2

    _, h = lax.associative_scan(combine, (a, u), axis=1)
    return h.astype(x.dtype)


def chunked_gated_delta_rule(q, k, v, g, beta):
    B, T, H, DK = q.shape
    DV = v.shape[-1]
    N = T // CHUNK
    q = l2_normalize(q) * (DK ** -0.5)
    k = l2_normalize(k)
    v = v.astype(jnp.float32)
    chunk = lambda t: t.reshape(B, N, CHUNK, H, t.shape[-1]).transpose(0, 3, 1, 2, 4)
    q, k, v = chunk(q), chunk(k), chunk(v)
    g = g.reshape(B, N, CHUNK, H).transpose(0, 3, 1, 2)
    beta = beta.reshape(B, N, CHUNK, H).transpose(0, 3, 1, 2)
    g = jnp.cumsum(g, axis=-1)
    causal = jnp.tril(jnp.ones((CHUNK, CHUNK), dtype=bool))
    strict = jnp.tril(jnp.ones((CHUNK, CHUNK), dtype=bool), -1)
    decay = jnp.exp(jnp.where(causal, g[..., :, None] - g[..., None, :], -jnp.inf))
    k_beta = k * beta[..., None]
    v_beta = v * beta[..., None]
    lower = jnp.where(strict, jnp.einsum('bhncd,bhnsd->bhncs', k_beta, k) * decay, 0.0)
    eye = jnp.eye(CHUNK, dtype=jnp.float32)
    rhs = jnp.concatenate([v_beta, k_beta * jnp.exp(g)[..., None]], axis=-1)
    sol = lax.linalg.triangular_solve(eye + lower, rhs, left_side=True, lower=True, unit_diagonal=True)
    u, w = sol[..., :DV], sol[..., DV:]
    attn = jnp.einsum('bhncd,bhnsd->bhncs', q, k) * decay
    lead = lambda t: jnp.moveaxis(t, 2, 0)
    xs = (lead(q), lead(k), lead(u), lead(w), lead(g), lead(attn))

    def step(S, inp):
        q_c, k_c, u_c, w_c, g_c, attn_c = inp
        v_new = u_c - jnp.einsum('bhck,bhkv->bhcv', w_c, S)
        o = (jnp.einsum('bhck,bhkv->bhcv', q_c * jnp.exp(g_c)[..., None], S)
             + jnp.einsum('bhcs,bhsv->bhcv', attn_c, v_new))
        g_last = g_c[..., -1]
        S = (S * jnp.exp(g_last)[..., None, None]
             + jnp.einsum('bhck,bhcv->bhkv', k_c * jnp.exp(g_last[..., None] - g_c)[..., None], v_new))
        return S, o

    S0 = jnp.zeros((B, H, DK, DV), jnp.float32)
    _, o = lax.scan(step, S0, xs)
    return o.transpose(1, 0, 3, 2, 4).reshape(B, T, H, DV)


def hybrid_mixer(h, w_in, lru_conv_w, lru_conv_b, lru_gate_a_w, lru_gate_a_b, lru_gate_i_w,
                 lru_gate_i_b, lru_lambda, gdn_conv_w, gdn_a_log, gdn_dt_bias, gdn_out_norm,
                 w_branch_lru, w_branch_gdn, w_out):
    B, T, _ = h.shape
    points, acc = [], 0
    for s in IN_SIZES[:-1]:
        acc += s
        points.append(acc)
    proj = h @ w_in
    lru_x, lru_gate, qkv, z, a_in, b_in, gate_lru, gate_gdn = jnp.split(proj, points, axis=-1)
    xa = causal_depthwise_conv(lru_x, lru_conv_w) + lru_conv_b
    ya = rg_lru(xa, lru_gate_a_w, lru_gate_a_b, lru_gate_i_w, lru_gate_i_b, lru_lambda) * jax.nn.gelu(lru_gate)
    qkv = jax.nn.silu(causal_depthwise_conv(qkv, gdn_conv_w))
    q, k, v = jnp.split(qkv, [GDN_KEY, 2 * GDN_KEY], axis=-1)
    q = q.reshape(B, T, GDN_HEADS, GDN_DK)
    k = k.reshape(B, T, GDN_HEADS, GDN_DK)
    v = v.reshape(B, T, GDN_HEADS, GDN_DV)
    beta = jax.nn.sigmoid(b_in.astype(jnp.float32))
    g = -jnp.exp(gdn_a_log.astype(jnp.float32)) * jax.nn.softplus(
        a_in.astype(jnp.float32) + gdn_dt_bias.astype(jnp.float32))
    o = chunked_gated_delta_rule(q, k, v, g, beta)
    o = rms_norm(o, gdn_out_norm) * jax.nn.silu(z.reshape(B, T, GDN_HEADS, GDN_DV).astype(jnp.float32))
    yb = o.reshape(B, T, GDN_VAL).astype(h.dtype)
    merged = (jax.nn.sigmoid(gate_lru) * (ya @ w_branch_lru)
              + jax.nn.sigmoid(gate_gdn) * (yb @ w_branch_gdn))
    return merged @ w_out


def setup_inputs(seed: int = 0) -> dict:
    key = jax.random.key(seed)
    ks = jax.random.split(key, 26)
    f32 = jnp.float32
    L = DEPTH

    def normal(k, shape, scale):
        return jax.random.normal(k, shape, f32) * scale

    a_pow = jax.random.uniform(ks[13], (L, LRU_WIDTH), f32, 0.9, 0.999)
    s = a_pow ** (1.0 / LRU_C)
    return {
        "x": normal(ks[0], (BATCH, SEQ, D_MODEL), 1.0),
        "c": normal(ks[1], (BATCH, D_MODEL), 1.0),
        "w_ada": normal(ks[2], (L, D_MODEL, 6 * D_MODEL), 0.5 * D_MODEL ** -0.5),
        "b_ada": normal(ks[3], (L, 6 * D_MODEL), 0.01),
        "mix_pre_norm": 1.0 + normal(ks[4], (L, D_MODEL), 0.02),
        "mix_post_norm": 1.0 + normal(ks[5], (L, D_MODEL), 0.02),
        "w_in": normal(ks[6], (L, D_MODEL, IN_WIDTH), D_MODEL ** -0.5),
        "lru_conv_w": normal(ks[7], (L, CONV_W, LRU_WIDTH), CONV_W ** -0.5),
        "lru_conv_b": normal(ks[8], (L, LRU_WIDTH), 0.01),
        "lru_gate_a_w": normal(ks[9], (L, LRU_BLOCKS, LRU_BLOCK, LRU_BLOCK), LRU_BLOCK ** -0.5),
        "lru_gate_a_b": normal(ks[10], (L, LRU_BLOCKS, LRU_BLOCK), 0.01),
        "lru_gate_i_w": normal(ks[11], (L, LRU_BLOCKS, LRU_BLOCK, LRU_BLOCK), LRU_BLOCK ** -0.5),
        "lru_gate_i_b": normal(ks[12], (L, LRU_BLOCKS, LRU_BLOCK), 0.01),
        "lru_lambda": jnp.log(s) - jnp.log1p(-s),
        "gdn_conv_w": normal(ks[14], (L, CONV_W, GDN_CONV_DIM), CONV_W ** -0.5),
        "gdn_a_log": jnp.log(jax.random.uniform(ks[15], (L, GDN_HEADS), f32, 1.0, 16.0)),
        "gdn_dt_bias": 1.0 + normal(ks[16], (L, GDN_HEADS), 0.1),
        "gdn_out_norm": 1.0 + normal(ks[17], (L, GDN_DV), 0.02),
        "w_branch_lru": normal(ks[18], (L, LRU_WIDTH, D_MODEL), LRU_WIDTH ** -0.5),
        "w_branch_gdn": normal(ks[19], (L, GDN_VAL, D_MODEL), GDN_VAL ** -0.5),
        "w_out": normal(ks[20], (L, D_MODEL, D_MODEL), D_MODEL ** -0.5),
        "mlp_pre_norm": 1.0 + normal(ks[21], (L, D_MODEL), 0.02),
        "mlp_post_norm": 1.0 + normal(ks[22], (L, D_MODEL), 0.02),
        "w_mlp_up": normal(ks[23], (L, D_MODEL, D_FF), D_MODEL ** -0.5),
        "w_mlp_down": normal(ks[24], (L, D_FF, D_MODEL), D_FF ** -0.5),
    }


def reference(x, c, w_ada, b_ada, mix_pre_norm, mix_post_norm, w_in, lru_conv_w, lru_conv_b,
              lru_gate_a_w, lru_gate_a_b, lru_gate_i_w, lru_gate_i_b, lru_lambda, gdn_conv_w,
              gdn_a_log, gdn_dt_bias, gdn_out_norm, w_branch_lru, w_branch_gdn, w_out,
              mlp_pre_norm, mlp_post_norm, w_mlp_up, w_mlp_down):
    c_act = jax.nn.silu(c)
    for l in range(DEPTH):
        mod = c_act @ w_ada[l] + b_ada[l]
        shift1, scale1, gate1, shift2, scale2, gate2 = [m[:, None, :] for m in jnp.split(mod, 6, axis=-1)]
        h = rms_norm(x, mix_pre_norm[l]) * (1.0 + scale1) + shift1
        y = hybrid_mixer(h, w_in[l], lru_conv_w[l], lru_conv_b[l], lru_gate_a_w[l], lru_gate_a_b[l],
                         lru_gate_i_w[l], lru_gate_i_b[l], lru_lambda[l], gdn_conv_w[l], gdn_a_log[l],
                         gdn_dt_bias[l], gdn_out_norm[l], w_branch_lru[l], w_branch_gdn[l], w_out[l])
        x = x + gate1 * rms_norm(y, mix_post_norm[l])
        h = rms_norm(x, mlp_pre_norm[l]) * (1.0 + scale2) + shift2
        y = jnp.square(jax.nn.relu(h @ w_mlp_up[l])) @ w_mlp_down[l]
        x = x + gate2 * rms_norm(y, mlp_post_norm[l])
    return x
```

```python
import functools

import jax
import jax.numpy as jnp
from jax import lax
from jax.experimental import pallas as pl
from jax.experimental.pallas import tpu as pltpu

F32 = jnp.float32
BF16 = jnp.bfloat16

EPS = 1e-6
CONV_W = 4
LRU_C = 8.0
V7X_LANES = 128
V7X_SUBLANES = 8
V7X_VMEM_LIMIT_BYTES = 56 * 1024 * 1024
GDN_CHUNK = 128
HIGHEST = lax.Precision.HIGHEST


def _params(*sem):
    return pltpu.CompilerParams(dimension_semantics=sem, vmem_limit_bytes=V7X_VMEM_LIMIT_BYTES)


def _sigmoid(x):
    return 1.0 / (1.0 + jnp.exp(-x))


def _silu(x):
    return x * _sigmoid(x)


def _softplus(x):
    return jnp.maximum(x, 0.0) + jnp.log1p(jnp.exp(-jnp.abs(x)))


def _gelu_tanh(x):
    c = 0.7978845608028654
    return 0.5 * x * (1.0 + jnp.tanh(c * (x + 0.044715 * (x * x * x))))


def _rms(x, w):
    return x * lax.rsqrt(jnp.mean(x * x, axis=-1, keepdims=True) + EPS) * w


def _causal_conv(prev8, cur, w):
    rows = cur.shape[0]
    xw = jnp.concatenate([prev8, cur], axis=0)
    out = w[3:4, :] * cur
    for back in (1, 2, 3):
        out = out + w[3 - back:4 - back, :] * pltpu.roll(xw, back, 0)[V7X_SUBLANES:V7X_SUBLANES + rows, :]
    return out


def _adaln_kernel(c_ref, w_ref, b_ref, o_ref):
    ca = _silu(c_ref[...])
    o_ref[...] = jnp.dot(ca.astype(BF16), w_ref[...].astype(BF16),
                         preferred_element_type=F32) + b_ref[...]


def _adaln(c_pad, w_ada, b_ada, tn=512):
    rows, d = c_pad.shape
    n = w_ada.shape[1]
    return pl.pallas_call(
        _adaln_kernel,
        grid=(n // tn,),
        in_specs=[pl.BlockSpec((rows, d), lambda j: (0, 0)),
                  pl.BlockSpec((d, tn), lambda j: (0, j)),
                  pl.BlockSpec((1, tn), lambda j: (0, j))],
        out_specs=pl.BlockSpec((rows, tn), lambda j: (0, j)),
        out_shape=jax.ShapeDtypeStruct((rows, n), F32),
        compiler_params=_params("arbitrary"),
        name="adaln",
    )(c_pad, w_ada, b_ada)


def _inproj_kernel(x_ref, mod_ref, nw_ref, w_ref, wab_ref, proj_ref, ab_ref, h_scr):
    @pl.when(pl.program_id(1) == 0)
    def _():
        h = _rms(x_ref[...], nw_ref[...]) * (1.0 + mod_ref[1:2, :]) + mod_ref[0:1, :]
        hb = h.astype(BF16)
        h_scr[...] = hb
        ab_ref[...] = jnp.dot(hb, wab_ref[...], preferred_element_type=F32)

    proj_ref[...] = jnp.dot(h_scr[...], w_ref[...], preferred_element_type=F32).astype(proj_ref.dtype)


def _inproj(x2, mod3, norm_w, w_main, w_ab, seq, tm, tn):
    m, d = x2.shape
    n = w_main.shape[1]
    nab = w_ab.shape[1]
    return pl.pallas_call(
        _inproj_kernel,
        grid=(m // tm, n // tn),
        in_specs=[pl.BlockSpec((tm, d), lambda i, j: (i, 0)),
                  pl.BlockSpec((None, 6, d), lambda i, j: ((i * tm) // seq, 0, 0)),
                  pl.BlockSpec((1, d), lambda i, j: (0, 0)),
                  pl.BlockSpec((d, tn), lambda i, j: (0, j)),
                  pl.BlockSpec((d, nab), lambda i, j: (0, 0))],
        out_specs=[pl.BlockSpec((tm, tn), lambda i, j: (i, j)),
                   pl.BlockSpec((tm, nab), lambda i, j: (i, 0))],
        out_shape=[jax.ShapeDtypeStruct((m, n), F32),
                   jax.ShapeDtypeStruct((m, nab), F32)],
        scratch_shapes=[pltpu.VMEM((tm, d), BF16)],
        compiler_params=_params("arbitrary", "arbitrary"),
        name="inproj",
    )(x2, mod3, norm_w, w_main, w_ab)


def _lru_kernel(x_ref, gate_ref, cw_ref, cb_ref, wai_ref, bai_ref, lam_ref, o_ref,
                tail_scr, h_scr, *, groups, tb):
    blk = V7X_LANES

    @pl.when(pl.program_id(2) == 0)
    def _():
        tail_scr[...] = jnp.zeros_like(tail_scr)
        h_scr[...] = jnp.zeros_like(h_scr)

    x = x_ref[...]
    xa = _causal_conv(tail_scr[...], x, cw_ref[...]) + cb_ref[...]
    tail_scr[...] = x[tb - V7X_SUBLANES:, :]
    sp = _softplus(-lam_ref[...])
    row = lax.broadcasted_iota(jnp.int32, (tb, blk), 0)
    for g in range(groups):
        sl = slice(g * blk, (g + 1) * blk)
        xg = xa[:, sl]
        ri = jnp.dot(xg.astype(BF16), wai_ref[g], preferred_element_type=F32) + bai_ref[g]
        r = _sigmoid(ri[:, :blk])
        i = _sigmoid(ri[:, blk:])
        log_a = (-LRU_C) * r * sp[:, sl]
        a = jnp.exp(log_a)
        u = jnp.sqrt(-jnp.tanh(log_a) * (1.0 + a * a)) * (i * xg)
        s = 1
        while s < tb:
            keep = row >= s
            a_sh = jnp.where(keep, pltpu.roll(a, s, 0), 1.0)
            u_sh = jnp.where(keep, pltpu.roll(u, s, 0), 0.0)
            u = a * u_sh + u
            a = a * a_sh
            s *= 2
        h = u + a * h_scr[0:1, sl]
        h_scr[0:1, sl] = h[tb - 1:tb, :]
        o_ref[:, sl] = (h * _gelu_tanh(gate_ref[:, sl])).astype(o_ref.dtype)


def _lru(proj3, conv_w, conv_b, w_ai, b_ai, lam, lw, groups, tb):
    b, t, _ = proj3.shape
    blk = V7X_LANES
    w = groups * blk
    ncb = lw // w
    kern = functools.partial(_lru_kernel, groups=groups, tb=tb)
    return pl.pallas_call(
        kern,
        grid=(b, ncb, t // tb),
        in_specs=[pl.BlockSpec((None, tb, w), lambda bi, ci, ti: (bi, ti, ci)),
                  pl.BlockSpec((None, tb, w), lambda bi, ci, ti: (bi, ti, ncb + ci)),
                  pl.BlockSpec((CONV_W, w), lambda bi, ci, ti: (0, ci)),
                  pl.BlockSpec((1, w), lambda bi, ci, ti: (0, ci)),
                  pl.BlockSpec((groups, blk, 2 * blk), lambda bi, ci, ti: (ci, 0, 0)),
                  pl.BlockSpec((groups, 1, 2 * blk), lambda bi, ci, ti: (ci, 0, 0)),
                  pl.BlockSpec((1, w), lambda bi, ci, ti: (0, ci))],
        out_specs=pl.BlockSpec((None, tb, w), lambda bi, ci, ti: (bi, ti, ci)),
        out_shape=jax.ShapeDtypeStruct((b, t, lw), BF16),
        scratch_shapes=[pltpu.VMEM((V7X_SUBLANES, w), F32), pltpu.VMEM((V7X_SUBLANES, w), F32)],
        compiler_params=_params("arbitrary", "arbitrary", "arbitrary"),
        name="rglru",
    )(proj3, proj3, conv_w, conv_b, w_ai, b_ai, lam)


def _gdn_prep_kernel(ab_ref, alog_ref, dtb_ref, gcol_ref, grow_ref, *, heads, chunk):
    t = ab_ref.shape[0]
    ab = ab_ref[...]
    g = -jnp.exp(alog_ref[...]) * _softplus(ab + dtb_ref[...])
    lane = lax.broadcasted_iota(jnp.int32, ab.shape, 1)
    row = lax.broadcasted_iota(jnp.int32, ab.shape, 0)
    pos = row & (chunk - 1)
    s = 1
    while s < chunk:
        g = g + jnp.where(pos >= s, pltpu.roll(g, s, 0), 0.0)
        s *= 2
    gcol_ref[...] = jnp.where(lane < heads, g, _sigmoid(ab))
    for c in range(t // chunk):
        grow_ref[c] = jnp.transpose(g[c * chunk:(c + 1) * chunk, :])[0:2 * V7X_SUBLANES, :]


def _gdn_prep(ab3, alog_pad, dtb_pad, heads, chunk):
    b, t, lanes = ab3.shape
    assert chunk == lanes and heads <= 2 * V7X_SUBLANES
    kern = functools.partial(_gdn_prep_kernel, heads=heads, chunk=chunk)
    return pl.pallas_call(
        kern,
        grid=(b,),
        in_specs=[pl.BlockSpec((None, t, lanes), lambda bi: (bi, 0, 0)),
                  pl.BlockSpec((1, lanes), lambda bi: (0, 0)),
                  pl.BlockSpec((1, lanes), lambda bi: (0, 0))],
        out_specs=[pl.BlockSpec((None, t, lanes), lambda bi: (bi, 0, 0)),
                   pl.BlockSpec((None, t // chunk, 2 * V7X_SUBLANES, chunk), lambda bi: (bi, 0, 0, 0))],
        out_shape=[jax.ShapeDtypeStruct((b, t, lanes), F32),
                   jax.ShapeDtypeStruct((b, t // chunk, 2 * V7X_SUBLANES, chunk), F32)],
        compiler_params=_params("arbitrary"),
        name="gdn_prep",
    )(ab3, alog_pad, dtb_pad)


def _dot_nt(a, b):
    return lax.dot_general(a, b, (((1,), (1,)), ((), ())), preferred_element_type=F32)


def _unit_lower_inverse(a_strict):
    n = a_strict.shape[0]
    ri = lax.broadcasted_iota(jnp.int32, (n, n), 0)
    ci = lax.broadcasted_iota(jnp.int32, (n, n), 1)
    x = -a_strict
    p = jnp.where(ri == ci, 1.0, 0.0) + x
    span = 2
    while span < n:
        x = jnp.dot(x, x, preferred_element_type=F32, precision=HIGHEST)
        p = p + jnp.dot(p, x, preferred_element_type=F32, precision=HIGHEST)
        span *= 2
    return p


def _gdn_kernel(q_ref, k_ref, v_ref, z_ref, cwq_ref, cwk_ref, cwv_ref, gcol_ref, grow_ref,
                nw_ref, o_ref, s_scr, *, heads, chunk, dk):
    c_len = chunk
    t = q_ref.shape[0]
    head = pl.program_id(1)
    s_scr[...] = jnp.zeros_like(s_scr)
    lane = lax.broadcasted_iota(jnp.int32, (c_len, V7X_LANES), 1)
    ri = lax.broadcasted_iota(jnp.int32, (c_len, c_len), 0)
    ci = lax.broadcasted_iota(jnp.int32, (c_len, c_len), 1)
    q_scale = dk ** -0.5

    def conv_silu(ref, cw_ref, start, prev_start, first):
        cur = ref[pl.ds(start, c_len), :]
        prev = jnp.where(first, 0.0, ref[pl.ds(prev_start, V7X_SUBLANES), :])
        return _silu(_causal_conv(prev, cur, cw_ref[...]))

    def body(c, carry):
        start = pl.multiple_of(c * c_len, c_len)
        prev_start = pl.multiple_of(jnp.maximum(start - V7X_SUBLANES, 0), V7X_SUBLANES)
        first = c == 0
        q = conv_silu(q_ref, cwq_ref, start, prev_start, first)
        k = conv_silu(k_ref, cwk_ref, start, prev_start, first)
        v = conv_silu(v_ref, cwv_ref, start, prev_start, first)
        q = q * (lax.rsqrt(jnp.sum(q * q, axis=-1, keepdims=True) + EPS) * q_scale)
        k = k * lax.rsqrt(jnp.sum(k * k, axis=-1, keepdims=True) + EPS)

        gb = gcol_ref[pl.ds(start, c_len), :]
        g_col = jnp.sum(jnp.where(lane == head, gb, 0.0), axis=-1, keepdims=True)
        b_col = jnp.sum(jnp.where(lane == head + heads, gb, 0.0), axis=-1, keepdims=True)
        g_row = grow_ref[c, pl.ds(head, 1), :]
        decay = jnp.exp(jnp.where(ri >= ci, g_col - g_row, -jnp.inf))
        e_g = jnp.exp(g_col)
        g_last = g_col[c_len - 1:c_len, :]

        kb = k * b_col
        vb = v * b_col
        kbf = k.astype(BF16)
        lower = jnp.where(ri > ci, _dot_nt(kb.astype(BF16), kbf) * decay, 0.0)
        t_inv = _unit_lower_inverse(lower)
        u = jnp.dot(t_inv, vb, preferred_element_type=F32, precision=HIGHEST)
        w = jnp.dot(t_inv, kb * e_g, preferred_element_type=F32, precision=HIGHEST)
        attn = _dot_nt(q.astype(BF16), kbf) * decay

        s_bf = s_scr[...].astype(BF16)
        v_new = u - jnp.dot(w.astype(BF16), s_bf, preferred_element_type=F32)
        o = (jnp.dot((q * e_g).astype(BF16), s_bf, preferred_element_type=F32)
             + jnp.dot(attn.astype(BF16), v_new.astype(BF16), preferred_element_type=F32))
        kd_t = jnp.transpose(k * jnp.exp(g_last - g_col))
        s_scr[...] = (s_scr[...] * jnp.exp(g_last)
                      + jnp.dot(kd_t.astype(BF16), v_new.astype(BF16), preferred_element_type=F32))

        o = _rms(o, nw_ref[...]) * _silu(z_ref[pl.ds(start, c_len), :])
        o_ref[pl.ds(start, c_len), :] = o.astype(o_ref.dtype)
        return carry

    lax.fori_loop(0, t // c_len, body, 0)


def _gdn(proj3, conv_w, gcol, grow, norm_w, heads, dk, dv, q_off, chunk):
    b, t, _ = proj3.shape
    assert dk == V7X_LANES and dv == V7X_LANES
    qb, kb, vb, zb = (q_off // dk, q_off // dk + heads, q_off // dk + 2 * heads, q_off // dk + 3 * heads)
    kern = functools.partial(_gdn_kernel, heads=heads, chunk=chunk, dk=dk)

    def col(base):
        return pl.BlockSpec((None, t, dk), lambda bi, hi: (bi, 0, base + hi))

    def cw(base):
        return pl.BlockSpec((CONV_W, dk), lambda bi, hi: (0, base + hi))

    return pl.pallas_call(
        kern,
        grid=(b, heads),
        in_specs=[col(qb), col(kb), col(vb), col(zb), cw(0), cw(heads), cw(2 * heads),
                  pl.BlockSpec((None, t, V7X_LANES), lambda bi, hi: (bi, 0, 0)),
                  pl.BlockSpec((None, t // chunk, 2 * V7X_SUBLANES, chunk), lambda bi, hi: (bi, 0, 0, 0)),
                  pl.BlockSpec((1, dv), lambda bi, hi: (0, 0))],
        out_specs=pl.BlockSpec((None, t, dv), lambda bi, hi: (bi, 0, hi)),
        out_shape=jax.ShapeDtypeStruct((b, t, heads * dv), BF16),
        scratch_shapes=[pltpu.VMEM((dk, dv), F32)],
        compiler_params=_params("arbitrary", "arbitrary"),
        name="gdn",
    )(proj3, proj3, proj3, proj3, conv_w, conv_w, conv_w, gcol, grow, norm_w)


def _merge_kernel(ya_ref, yb_ref, wl_ref, wg_ref, gl_ref, gg_ref, o_ref):
    pa = jnp.dot(ya_ref[...], wl_ref[...], preferred_element_type=F32)
    pb = jnp.dot(yb_ref[...], wg_ref[...], preferred_element_type=F32)
    o_ref[...] = (_sigmoid(gl_ref[...]) * pa + _sigmoid(gg_ref[...]) * pb).astype(o_ref.dtype)


def _merge(ya2, yb2, w_l, w_g, proj2, gate_off, tm, tn):
    m, ka = ya2.shape
    kb = yb2.shape[1]
    d = w_l.shape[1]
    gl0 = gate_off // tn
    gg0 = (gate_off + d) // tn
    return pl.pallas_call(
        _merge_kernel,
        grid=(m // tm, d // tn),
        in_specs=[pl.BlockSpec((tm, ka), lambda i, j: (i, 0)),
                  pl.BlockSpec((tm, kb), lambda i, j: (i, 0)),
                  pl.BlockSpec((ka, tn), lambda i, j: (0, j)),
                  pl.BlockSpec((kb, tn), lambda i, j: (0, j)),
                  pl.BlockSpec((tm, tn), lambda i, j: (i, gl0 + j)),
                  pl.BlockSpec((tm, tn), lambda i, j: (i, gg0 + j))],
        out_specs=pl.BlockSpec((tm, tn), lambda i, j: (i, j)),
        out_shape=jax.ShapeDtypeStruct((m, d), BF16),
        compiler_params=_params("arbitrary", "arbitrary"),
        name="merge",
    )(ya2, yb2, w_l, w_g, proj2, proj2)


def _mm_kernel(a_ref, w_ref, o_ref, *, relu2):
    y = jnp.dot(a_ref[...], w_ref[...], preferred_element_type=F32)
    if relu2:
        y = jnp.square(jnp.maximum(y, 0.0))
    o_ref[...] = y.astype(o_ref.dtype)


def _mm(a, w, out_dtype, tm, tn, relu2=False, name="mm"):
    m, k = a.shape
    n = w.shape[1]
    return pl.pallas_call(
        functools.partial(_mm_kernel, relu2=relu2),
        grid=(m // tm, n // tn),
        in_specs=[pl.BlockSpec((tm, k), lambda i, j: (i, 0)),
                  pl.BlockSpec((k, tn), lambda i, j: (0, j))],
        out_specs=pl.BlockSpec((tm, tn), lambda i, j: (i, j)),
        out_shape=jax.ShapeDtypeStruct((m, n), out_dtype),
        compiler_params=_params("arbitrary", "arbitrary"),
        name=name,
    )(a, w)


def _mm_acc_kernel(a_ref, w_ref, o_ref, acc_ref):
    kk = pl.program_id(2)

    @pl.when(kk == 0)
    def _():
        acc_ref[...] = jnp.zeros_like(acc_ref)

    acc_ref[...] += jnp.dot(a_ref[...], w_ref[...], preferred_element_type=F32)

    @pl.when(kk == pl.num_programs(2) - 1)
    def _():
        o_ref[...] = acc_ref[...].astype(o_ref.dtype)


def _mm_acc(a, w, out_dtype, tm, tn, tk, name="mm_acc"):
    m, k = a.shape
    n = w.shape[1]
    return pl.pallas_call(
        _mm_acc_kernel,
        grid=(m // tm, n // tn, k // tk),
        in_specs=[pl.BlockSpec((tm, tk), lambda i, j, kk: (i, kk)),
                  pl.BlockSpec((tk, tn), lambda i, j, kk: (kk, j))],
        out_specs=pl.BlockSpec((tm, tn), lambda i, j, kk: (i, j)),
        out_shape=jax.ShapeDtypeStruct((m, n), out_dtype),
        scratch_shapes=[pltpu.VMEM((tm, tn), F32)],
        compiler_params=_params("arbitrary", "arbitrary", "arbitrary"),
        name=name,
    )(a, w)


def _resid_prenorm_kernel(y_ref, x_ref, mod_ref, pw_ref, nw_ref, x1_ref, h_ref):
    x1 = x_ref[...] + mod_ref[2:3, :] * _rms(y_ref[...], pw_ref[...])
    x1_ref[...] = x1
    h = _rms(x1, nw_ref[...]) * (1.0 + mod_ref[4:5, :]) + mod_ref[3:4, :]
    h_ref[...] = h.astype(h_ref.dtype)


def _resid_prenorm(y, x2, mod3, post_w, pre_w, seq, tm):
    m, d = x2.shape
    row = pl.BlockSpec((tm, d), lambda i: (i, 0))
    vec = pl.BlockSpec((1, d), lambda i: (0, 0))
    return pl.pallas_call(
        _resid_prenorm_kernel,
        grid=(m // tm,),
        in_specs=[row, row, pl.BlockSpec((None, 6, d), lambda i: ((i * tm) // seq, 0, 0)), vec, vec],
        out_specs=[row, row],
        out_shape=[jax.ShapeDtypeStruct((m, d), F32), jax.ShapeDtypeStruct((m, d), BF16)],
        compiler_params=_params("arbitrary"),
        name="resid_prenorm",
    )(y, x2, mod3, post_w, pre_w)


def _resid_kernel(y_ref, x_ref, mod_ref, pw_ref, o_ref):
    o_ref[...] = x_ref[...] + mod_ref[5:6, :] * _rms(y_ref[...], pw_ref[...])


def _resid(y, x1, mod3, post_w, seq, tm):
    m, d = x1.shape
    row = pl.BlockSpec((tm, d), lambda i: (i, 0))
    return pl.pallas_call(
        _resid_kernel,
        grid=(m // tm,),
        in_specs=[row, row, pl.BlockSpec((None, 6, d), lambda i: ((i * tm) // seq, 0, 0)),
                  pl.BlockSpec((1, d), lambda i: (0, 0))],
        out_specs=row,
        out_shape=jax.ShapeDtypeStruct((m, d), F32),
        compiler_params=_params("arbitrary"),
        name="resid",
    )(y, x1, mod3, post_w)


def _tile(n, pref):
    t = min(n, pref)
    while n % t:
        t //= 2
    return t


def _layer(x2, mod3, batch, seq, mix_pre_norm, mix_post_norm, w_in, lru_conv_w, lru_conv_b,
           lru_gate_a_w, lru_gate_a_b, lru_gate_i_w, lru_gate_i_b, lru_lambda, gdn_conv_w,
           gdn_a_log, gdn_dt_bias, gdn_out_norm, w_branch_lru, w_branch_gdn, w_out,
           mlp_pre_norm, mlp_post_norm, w_mlp_up, w_mlp_down):
    m, d = x2.shape
    lw = lru_lambda.shape[-1]
    nblk = lru_gate_a_w.shape[0]
    heads = gdn_a_log.shape[-1]
    dv = gdn_out_norm.shape[-1]
    val = heads * dv
    conv_dim = gdn_conv_w.shape[-1]
    dk = (conv_dim - val) // (2 * heads)
    lanes = V7X_LANES

    o_z = 2 * lw + conv_dim
    o_a = o_z + val
    o_gl = o_a + 2 * heads
    w_main = jnp.concatenate([w_in[:, :o_a], w_in[:, o_gl:]], axis=1).astype(BF16)
    w_ab = jnp.pad(w_in[:, o_a:o_gl], ((0, 0), (0, lanes - 2 * heads))).astype(BF16)
    n_main = w_main.shape[1]
    gate_off = o_a

    tm = _tile(m, 512)
    proj, ab = _inproj(x2, mod3, mix_pre_norm[None, :], w_main, w_ab, seq, tm, _tile(n_main, 1024))
    proj3 = proj.reshape(batch, seq, n_main)

    groups = 4 if nblk % 4 == 0 else 1
    w_ai = jnp.concatenate([lru_gate_a_w, lru_gate_i_w], axis=-1).astype(BF16)
    b_ai = jnp.concatenate([lru_gate_a_b, lru_gate_i_b], axis=-1)[:, None, :]
    ya = _lru(proj3, lru_conv_w, lru_conv_b[None, :], w_ai, b_ai, lru_lambda[None, :], lw, groups,
              _tile(seq, 512))

    pad_h = lambda v: jnp.pad(v, (0, lanes - heads))[None, :]
    gcol, grow = _gdn_prep(ab.reshape(batch, seq, lanes), pad_h(gdn_a_log), pad_h(gdn_dt_bias),
                           heads, GDN_CHUNK)
    yb = _gdn(proj3, gdn_conv_w, gcol, grow, gdn_out_norm[None, :], heads, dk, dv, 2 * lw, GDN_CHUNK)

    merged = _merge(ya.reshape(m, lw), yb.reshape(m, val), w_branch_lru.astype(BF16),
                    w_branch_gdn.astype(BF16), proj, gate_off, tm, _tile(d, 1024))
    tm_big = _tile(m, 1024)
    y1 = _mm(merged, w_out.astype(BF16), F32, tm_big, _tile(d, 1024), name="out_proj")
    x1, h2 = _resid_prenorm(y1, x2, mod3, mix_post_norm[None, :], mlp_pre_norm[None, :], seq, _tile(m, 256))

    d_ff = w_mlp_up.shape[1]
    hid = _mm(h2, w_mlp_up.astype(BF16), BF16, tm_big, _tile(d_ff, 1024), relu2=True, name="mlp_up")
    y2 = _mm_acc(hid, w_mlp_down.astype(BF16), F32, tm_big, _tile(d, 2048), _tile(d_ff, 2048), name="mlp_down")
    return _resid(y2, x1, mod3, mlp_post_norm[None, :], seq, _tile(m, 256))


def kernel(x, c, w_ada, b_ada, mix_pre_norm, mix_post_norm, w_in, lru_conv_w, lru_conv_b, lru_gate_a_w, lru_gate_a_b, lru_gate_i_w, lru_gate_i_b, lru_lambda, gdn_conv_w, gdn_a_log, gdn_dt_bias, gdn_out_norm, w_branch_lru, w_branch_gdn, w_out, mlp_pre_norm, mlp_post_norm, w_mlp_up, w_mlp_down):
    batch, seq, d = x.shape
    depth = w_ada.shape[0]
    x2 = x.reshape(batch * seq, d)
    c_pad = jnp.pad(c, ((0, V7X_SUBLANES - batch % V7X_SUBLANES), (0, 0))) if batch % V7X_SUBLANES else c
    for l in range(depth):
        mod = _adaln(c_pad, w_ada[l], b_ada[l][None, :], _tile(6 * d, 512))
        mod3 = mod[:batch].reshape(batch, 6, d)
        x2 = _layer(x2, mod3, batch, seq, mix_pre_norm[l], mix_post_norm[l], w_in[l], lru_conv_w[l],
                    lru_conv_b[l], lru_gate_a_w[l], lru_gate_a_b[l], lru_gate_i_w[l], lru_gate_i_b[l],
                    lru_lambda[l], gdn_conv_w[l], gdn_a_log[l], gdn_dt_bias[l], gdn_out_norm[l],
                    w_branch_lru[l], w_branch_gdn[l], w_out[l], mlp_pre_norm[l], mlp_post_norm[l],
                    w_mlp_up[l], w_mlp_down[l])
    return x2.reshape(batch, seq, d)
```

```python
import functools

import jax
import jax.numpy as jnp
from jax import lax
from jax.experimental import pallas as pl
from jax.experimental.pallas import tpu as pltpu

F32 = jnp.float32
BF16 = jnp.bfloat16

EPS = 1e-6
CONV_W = 4
LRU_C = 8.0
V7X_LANES = 128
V7X_SUBLANES = 8
V7X_VMEM_LIMIT_BYTES = 56 * 1024 * 1024
GDN_CHUNK = 128
GDN_HEADS_PER_STEP = 4
GDN_ROWS_PER_STEP = 512


def _params(*sem):
    return pltpu.CompilerParams(dimension_semantics=sem, vmem_limit_bytes=V7X_VMEM_LIMIT_BYTES)


def _sigmoid(x):
    return 1.0 / (1.0 + jnp.exp(-x))


def _silu(x):
    return x * _sigmoid(x)


def _softplus(x):
    return jnp.maximum(x, 0.0) + jnp.log1p(jnp.exp(-jnp.abs(x)))


def _gelu_tanh(x):
    c = 0.7978845608028654
    return 0.5 * x * (1.0 + jnp.tanh(c * (x + 0.044715 * (x * x * x))))


def _rms(x, w):
    return x * lax.rsqrt(jnp.mean(x * x, axis=-1, keepdims=True) + EPS) * w


def _causal_conv(prev8, cur, w):
    rows = cur.shape[0]
    xw = jnp.concatenate([prev8, cur], axis=0)
    out = w[3:4, :] * cur
    for back in (1, 2, 3):
        out = out + w[3 - back:4 - back, :] * pltpu.roll(xw, back, 0)[V7X_SUBLANES:V7X_SUBLANES + rows, :]
    return out


def _dot(a, b):
    return jnp.dot(a, b, preferred_element_type=F32)


def _dot_split(a, b):
    a_hi = a.astype(BF16)
    b_hi = b.astype(BF16)
    a_lo = (a - a_hi.astype(F32)).astype(BF16)
    b_lo = (b - b_hi.astype(F32)).astype(BF16)
    return (_dot(a_lo, b_hi) + _dot(a_hi, b_lo)) + _dot(a_hi, b_hi)


def _dot_nt(a, b):
    return lax.dot_general(a, b, (((1,), (1,)), ((), ())), preferred_element_type=F32)


def _adaln_kernel(c_ref, w_ref, b_ref, o_ref):
    ca = _silu(c_ref[...])
    o_ref[...] = _dot(ca.astype(BF16), w_ref[...].astype(BF16)) + b_ref[...]


def _adaln(c_pad, w_ada, b_ada, tn):
    rows, d = c_pad.shape
    n = w_ada.shape[1]
    return pl.pallas_call(
        _adaln_kernel,
        grid=(n // tn,),
        in_specs=[pl.BlockSpec((rows, d), lambda j: (0, 0)),
                  pl.BlockSpec((d, tn), lambda j: (0, j)),
                  pl.BlockSpec((1, tn), lambda j: (0, j))],
        out_specs=pl.BlockSpec((rows, tn), lambda j: (0, j)),
        out_shape=jax.ShapeDtypeStruct((rows, n), F32),
        compiler_params=_params("arbitrary"),
        name="adaln",
    )(c_pad, w_ada, b_ada)


def _wprep_kernel(a_ref, b_ref, o_ref, *, n_aligned, shift):
    j = pl.program_id(1)

    @pl.when(j < n_aligned)
    def _():
        o_ref[...] = a_ref[...].astype(o_ref.dtype)

    @pl.when(j >= n_aligned)
    def _():
        tn = a_ref.shape[1]
        lanes = b_ref.shape[1]
        moved = pltpu.roll(a_ref[...], tn - shift, 1)
        o_ref[:, :tn - lanes] = moved[:, :tn - lanes].astype(o_ref.dtype)
        lane = lax.broadcasted_iota(jnp.int32, b_ref.shape, 1)
        tail = jnp.where(lane < lanes - shift, moved[:, tn - lanes:], pltpu.roll(b_ref[...], lanes - shift, 1))
        o_ref[:, tn - lanes:] = tail.astype(o_ref.dtype)


def _wprep(w_in, n_out, n_aligned_cols, shift, tk, tn):
    k, n_src = w_in.shape
    lanes = V7X_LANES
    assert n_aligned_cols % tn == 0 and n_out % tn == 0 and 0 < shift < lanes
    last_b = (n_src - 1) // lanes
    kern = functools.partial(_wprep_kernel, n_aligned=n_aligned_cols // tn, shift=shift)
    return pl.pallas_call(
        kern,
        grid=(k // tk, n_out // tn),
        in_specs=[pl.BlockSpec((tk, tn), lambda i, j: (i, j)),
                  pl.BlockSpec((tk, lanes), lambda i, j: (i, jnp.minimum((j + 1) * (tn // lanes), last_b)))],
        out_specs=pl.BlockSpec((tk, tn), lambda i, j: (i, j)),
        out_shape=jax.ShapeDtypeStruct((k, n_out), BF16),
        compiler_params=_params("arbitrary", "arbitrary"),
        name="wprep",
    )(w_in, w_in)


def _inproj_kernel(x_ref, mod_ref, nw_ref, w_ref, wab_ref, proj_ref, ab_ref, h_scr):
    @pl.when(pl.program_id(1) == 0)
    def _():
        h = _rms(x_ref[...], nw_ref[...]) * (1.0 + mod_ref[1:2, :]) + mod_ref[0:1, :]
        hb = h.astype(BF16)
        h_scr[...] = hb
        ab_ref[...] = _dot(hb, wab_ref[...])

    proj_ref[...] = _dot(h_scr[...], w_ref[...]).astype(proj_ref.dtype)


def _inproj(x2, mod3, norm_w, w_main, w_ab, seq, tm, tn):
    m, d = x2.shape
    n = w_main.shape[1]
    nab = w_ab.shape[1]
    return pl.pallas_call(
        _inproj_kernel,
        grid=(m // tm, n // tn),
        in_specs=[pl.BlockSpec((tm, d), lambda i, j: (i, 0)),
                  pl.BlockSpec((None, 6, d), lambda i, j: ((i * tm) // seq, 0, 0)),
                  pl.BlockSpec((1, d), lambda i, j: (0, 0)),
                  pl.BlockSpec((d, tn), lambda i, j: (0, j)),
                  pl.BlockSpec((d, nab), lambda i, j: (0, 0))],
        out_specs=[pl.BlockSpec((tm, tn), lambda i, j: (i, j)),
                   pl.BlockSpec((tm, nab), lambda i, j: (i, 0))],
        out_shape=[jax.ShapeDtypeStruct((m, n), BF16),
                   jax.ShapeDtypeStruct((m, nab), F32)],
        scratch_shapes=[pltpu.VMEM((tm, d), BF16)],
        compiler_params=_params("arbitrary", "arbitrary"),
        name="inproj",
    )(x2, mod3, norm_w, w_main, w_ab)


def _lru_kernel(x_ref, gate_ref, cw_ref, cb_ref, wai_ref, bai_ref, lam_ref, o_ref,
                tail_scr, h_scr, *, groups, tb):
    blk = V7X_LANES

    @pl.when(pl.program_id(2) == 0)
    def _():
        tail_scr[...] = jnp.zeros_like(tail_scr)
        h_scr[...] = jnp.zeros_like(h_scr)

    x = x_ref[...].astype(F32)
    xa = _causal_conv(tail_scr[...], x, cw_ref[...]) + cb_ref[...]
    tail_scr[...] = x[tb - V7X_SUBLANES:, :]
    sp = _softplus(-lam_ref[...])
    row = lax.broadcasted_iota(jnp.int32, (tb, blk), 0)
    for g in range(groups):
        sl = slice(g * blk, (g + 1) * blk)
        xg = xa[:, sl]
        ri = _dot(xg.astype(BF16), wai_ref[g]) + bai_ref[g]
        r = _sigmoid(ri[:, :blk])
        i = _sigmoid(ri[:, blk:])
        log_a = (-LRU_C) * r * sp[:, sl]
        a = jnp.exp(log_a)
        u = jnp.sqrt(-jnp.tanh(log_a) * (1.0 + a * a)) * (i * xg)
        s = 1
        while s < tb:
            keep = row >= s
            a_sh = jnp.where(keep, pltpu.roll(a, s, 0), 1.0)
            u_sh = jnp.where(keep, pltpu.roll(u, s, 0), 0.0)
            u = a * u_sh + u
            a = a * a_sh
            s *= 2
        h = u + a * h_scr[0:1, sl]
        h_scr[0:1, sl] = h[tb - 1:tb, :]
        o_ref[:, sl] = (h * _gelu_tanh(gate_ref[:, sl].astype(F32))).astype(o_ref.dtype)


def _lru(proj3, conv_w, conv_b, w_ai, b_ai, lam, lw, groups, tb):
    b, t, _ = proj3.shape
    blk = V7X_LANES
    w = groups * blk
    ncb = lw // w
    kern = functools.partial(_lru_kernel, groups=groups, tb=tb)
    return pl.pallas_call(
        kern,
        grid=(b, ncb, t // tb),
        in_specs=[pl.BlockSpec((None, tb, w), lambda bi, ci, ti: (bi, ti, ci)),
                  pl.BlockSpec((None, tb, w), lambda bi, ci, ti: (bi, ti, ncb + ci)),
                  pl.BlockSpec((CONV_W, w), lambda bi, ci, ti: (0, ci)),
                  pl.BlockSpec((1, w), lambda bi, ci, ti: (0, ci)),
                  pl.BlockSpec((groups, blk, 2 * blk), lambda bi, ci, ti: (ci, 0, 0)),
                  pl.BlockSpec((groups, 1, 2 * blk), lambda bi, ci, ti: (ci, 0, 0)),
                  pl.BlockSpec((1, w), lambda bi, ci, ti: (0, ci))],
        out_specs=pl.BlockSpec((None, tb, w), lambda bi, ci, ti: (bi, ti, ci)),
        out_shape=jax.ShapeDtypeStruct((b, t, lw), BF16),
        scratch_shapes=[pltpu.VMEM((V7X_SUBLANES, w), F32), pltpu.VMEM((V7X_SUBLANES, w), F32)],
        compiler_params=_params("arbitrary", "arbitrary", "arbitrary"),
        name="rglru",
    )(proj3, proj3, conv_w, conv_b, w_ai, b_ai, lam)


def _gdn_prep_kernel(ab_ref, alog_ref, dtb_ref, gcol_ref, grow_ref, *, heads, chunk):
    t = ab_ref.shape[0]
    ab = ab_ref[...]
    g = -jnp.exp(alog_ref[...]) * _softplus(ab + dtb_ref[...])
    lane = lax.broadcasted_iota(jnp.int32, ab.shape, 1)
    row = lax.broadcasted_iota(jnp.int32, ab.shape, 0)
    pos = row & (chunk - 1)
    s = 1
    while s < chunk:
        g = g + jnp.where(pos >= s, pltpu.roll(g, s, 0), 0.0)
        s *= 2
    gcol_ref[...] = jnp.where(lane < heads, g, _sigmoid(ab))
    for c in range(t // chunk):
        grow_ref[c] = jnp.transpose(g[c * chunk:(c + 1) * chunk, :])[0:2 * V7X_SUBLANES, :]


def _gdn_prep(ab3, alog_pad, dtb_pad, heads, chunk):
    b, t, lanes = ab3.shape
    assert chunk == lanes and heads <= 2 * V7X_SUBLANES
    kern = functools.partial(_gdn_prep_kernel, heads=heads, chunk=chunk)
    return pl.pallas_call(
        kern,
        grid=(b,),
        in_specs=[pl.BlockSpec((None, t, lanes), lambda bi: (bi, 0, 0)),
                  pl.BlockSpec((1, lanes), lambda bi: (0, 0)),
                  pl.BlockSpec((1, lanes), lambda bi: (0, 0))],
        out_specs=[pl.BlockSpec((None, t, lanes), lambda bi: (bi, 0, 0)),
                   pl.BlockSpec((None, t // chunk, 2 * V7X_SUBLANES, chunk), lambda bi: (bi, 0, 0, 0))],
        out_shape=[jax.ShapeDtypeStruct((b, t, lanes), F32),
                   jax.ShapeDtypeStruct((b, t // chunk, 2 * V7X_SUBLANES, chunk), F32)],
        compiler_params=_params("arbitrary"),
        name="gdn_prep",
    )(ab3, alog_pad, dtb_pad)


def _gdn_kernel(q_ref, k_ref, v_ref, z_ref, cwq_ref, cwk_ref, cwv_ref, gcol_ref, grow_ref,
                nw_ref, o_ref, s_scr, tq_scr, tk_scr, tv_scr, *, heads, nh, chunk, dk):
    c_len = chunk
    tb = q_ref.shape[0]
    n_chunks = tb // c_len
    head0 = pl.program_id(1) * nh

    @pl.when(pl.program_id(2) == 0)
    def _():
        s_scr[...] = jnp.zeros_like(s_scr)
        tq_scr[...] = jnp.zeros_like(tq_scr)
        tk_scr[...] = jnp.zeros_like(tk_scr)
        tv_scr[...] = jnp.zeros_like(tv_scr)

    def conv_silu(ref, cw_ref, tail_scr):
        x = ref[...].astype(F32)
        y = _silu(_causal_conv(tail_scr[...], x, cw_ref[...]))
        tail_scr[...] = x[tb - V7X_SUBLANES:, :]
        return y

    q_all = conv_silu(q_ref, cwq_ref, tq_scr)
    k_all = conv_silu(k_ref, cwk_ref, tk_scr)
    v_all = conv_silu(v_ref, cwv_ref, tv_scr)

    lane = lax.broadcasted_iota(jnp.int32, (c_len, V7X_LANES), 1)
    ri = lax.broadcasted_iota(jnp.int32, (c_len, c_len), 0)
    ci = lax.broadcasted_iota(jnp.int32, (c_len, c_len), 1)
    q_scale = dk ** -0.5

    pairs = [(c, h) for c in range(n_chunks) for h in range(nh)]
    loc = {}
    for c, h in pairs:
        rows = slice(c * c_len, (c + 1) * c_len)
        cols = slice(h * dk, (h + 1) * dk)
        head = head0 + h
        gb = gcol_ref[rows, :]
        q = q_all[rows, cols]
        k = k_all[rows, cols]
        q = q * (lax.rsqrt(jnp.sum(q * q, axis=-1, keepdims=True) + EPS) * q_scale)
        k = k * lax.rsqrt(jnp.sum(k * k, axis=-1, keepdims=True) + EPS)
        g_col = jnp.sum(jnp.where(lane == head, gb, 0.0), axis=-1, keepdims=True)
        b_col = jnp.sum(jnp.where(lane == head + heads, gb, 0.0), axis=-1, keepdims=True)
        g_row = grow_ref[c, pl.ds(head, 1), :]
        e_g = jnp.exp(g_col)
        g_last = g_col[c_len - 1:c_len, :]
        kb = k * b_col
        loc[c, h] = dict(
            decay=jnp.exp(jnp.where(ri >= ci, g_col - g_row, -jnp.inf)),
            kq_lhs=jnp.concatenate([kb, q], axis=0).astype(BF16), k_bf=k.astype(BF16),
            rhs=jnp.concatenate([v_all[rows, cols] * b_col, kb * e_g], axis=1),
            qe=q * e_g, kd=k * jnp.exp(g_last - g_col), e_last=jnp.exp(g_last))

    for p in pairs:
        loc[p]["kq"] = _dot_nt(loc[p]["kq_lhs"], loc[p]["k_bf"])
    for p in pairs:
        d = loc[p]
        d["x"] = jnp.where(ri > ci, -(d["kq"][:c_len] * d["decay"]), 0.0)
        d["attn"] = d["kq"][c_len:] * d["decay"]
        d["pow"] = d["x"]
        d["t_off"] = d["x"]
    span = 2
    while span < c_len:
        for p in pairs:
            xb = loc[p]["pow"].astype(BF16)
            loc[p]["pow"] = _dot(xb, xb)
        for p in pairs:
            d = loc[p]
            d["t_off"] = d["t_off"] + d["pow"] + _dot(d["t_off"].astype(BF16), d["pow"].astype(BF16))
        span *= 2
    for p in pairs:
        d = loc[p]
        d["t_bf"] = d["t_off"].astype(BF16)
        d["sol"] = d["rhs"] + _dot(d["t_bf"], d["rhs"].astype(BF16))
    for p in pairs:
        d = loc[p]
        d["res"] = (d["rhs"] - d["sol"]) + _dot_split(d["x"], d["sol"])
    for p in pairs:
        d = loc[p]
        sol = d["sol"] + (d["res"] + _dot(d["t_bf"], d["res"].astype(BF16)))
        d["u"] = sol[:, :dk]
        d["lhs_s"] = jnp.concatenate([sol[:, dk:], d["qe"]], axis=0).astype(BF16)
        d["lhs_v"] = jnp.concatenate([d["attn"], jnp.transpose(d["kd"])], axis=0).astype(BF16)

    state = [s_scr[h] for h in range(nh)]
    for c in range(n_chunks):
        rows = slice(c * c_len, (c + 1) * c_len)
        r_s = [_dot(loc[c, h]["lhs_s"], state[h].astype(BF16)) for h in range(nh)]
        v_new = [loc[c, h]["u"] - r_s[h][:c_len] for h in range(nh)]
        r_v = [_dot(loc[c, h]["lhs_v"], v_new[h].astype(BF16)) for h in range(nh)]
        for h in range(nh):
            cols = slice(h * dk, (h + 1) * dk)
            state[h] = state[h] * loc[c, h]["e_last"] + r_v[h][c_len:]
            o = r_s[h][c_len:] + r_v[h][:c_len]
            o = _rms(o, nw_ref[...]) * _silu(z_ref[rows, cols].astype(F32))
            o_ref[rows, cols] = o.astype(o_ref.dtype)
    for h in range(nh):
        s_scr[h] = state[h]


def _gdn(proj3, conv_w, gcol, grow, norm_w, heads, dk, dv, q_off, chunk, nh, tb):
    b, t, _ = proj3.shape
    assert dk == V7X_LANES and dv == V7X_LANES and heads % nh == 0 and t % tb == 0 and tb % chunk == 0
    w = nh * dk
    hg = heads // nh
    qb = q_off // w
    kern = functools.partial(_gdn_kernel, heads=heads, nh=nh, chunk=chunk, dk=dk)

    def col(base):
        return pl.BlockSpec((None, tb, w), lambda bi, hi, ti: (bi, ti, base + hi))

    def cw(base):
        return pl.BlockSpec((CONV_W, w), lambda bi, hi, ti: (0, base + hi))

    tail = pltpu.VMEM((V7X_SUBLANES, w), F32)
    return pl.pallas_call(
        kern,
        grid=(b, hg, t // tb),
        in_specs=[col(qb), col(qb + hg), col(qb + 2 * hg), col(qb + 3 * hg), cw(0), cw(hg), cw(2 * hg),
                  pl.BlockSpec((None, tb, V7X_LANES), lambda bi, hi, ti: (bi, ti, 0)),
                  pl.BlockSpec((None, tb // chunk, 2 * V7X_SUBLANES, chunk), lambda bi, hi, ti: (bi, ti, 0, 0)),
                  pl.BlockSpec((1, dv), lambda bi, hi, ti: (0, 0))],
        out_specs=pl.BlockSpec((None, tb, w), lambda bi, hi, ti: (bi, ti, hi)),
        out_shape=jax.ShapeDtypeStruct((b, t, heads * dv), BF16),
        scratch_shapes=[pltpu.VMEM((nh, dk, dv), F32), tail, tail, tail],
        compiler_params=_params("arbitrary", "arbitrary", "arbitrary"),
        name="gdn",
    )(proj3, proj3, proj3, proj3, conv_w, conv_w, conv_w, gcol, grow, norm_w)


def _merge_kernel(ya_ref, yb_ref, wl_ref, wg_ref, gl_ref, gg_ref, o_ref):
    pa = _dot(ya_ref[...], wl_ref[...])
    pb = _dot(yb_ref[...], wg_ref[...])
    o_ref[...] = (_sigmoid(gl_ref[...].astype(F32)) * pa
                  + _sigmoid(gg_ref[...].astype(F32)) * pb).astype(o_ref.dtype)


def _merge(ya2, yb2, w_l, w_g, proj2, gate_off, tm, tn):
    m, ka = ya2.shape
    kb = yb2.shape[1]
    d = w_l.shape[1]
    assert gate_off % tn == 0 and d % tn == 0
    gl0 = gate_off // tn
    gg0 = (gate_off + d) // tn
    return pl.pallas_call(
        _merge_kernel,
        grid=(m // tm, d // tn),
        in_specs=[pl.BlockSpec((tm, ka), lambda i, j: (i, 0)),
                  pl.BlockSpec((tm, kb), lambda i, j: (i, 0)),
                  pl.BlockSpec((ka, tn), lambda i, j: (0, j)),
                  pl.BlockSpec((kb, tn), lambda i, j: (0, j)),
                  pl.BlockSpec((tm, tn), lambda i, j: (i, gl0 + j)),
                  pl.BlockSpec((tm, tn), lambda i, j: (i, gg0 + j))],
        out_specs=pl.BlockSpec((tm, tn), lambda i, j: (i, j)),
        out_shape=jax.ShapeDtypeStruct((m, d), BF16),
        compiler_params=_params("arbitrary", "arbitrary"),
        name="merge",
    )(ya2, yb2, w_l, w_g, proj2, proj2)


def _mm_kernel(a_ref, w_ref, o_ref, *, relu2):
    y = _dot(a_ref[...], w_ref[...])
    if relu2:
        y = jnp.square(jnp.maximum(y, 0.0))
    o_ref[...] = y.astype(o_ref.dtype)


def _mm(a, w, out_dtype, tm, tn, relu2=False, name="mm"):
    m, k = a.shape
    n = w.shape[1]
    return pl.pallas_call(
        functools.partial(_mm_kernel, relu2=relu2),
        grid=(m // tm, n // tn),
        in_specs=[pl.BlockSpec((tm, k), lambda i, j: (i, 0)),
                  pl.BlockSpec((k, tn), lambda i, j: (0, j))],
        out_specs=pl.BlockSpec((tm, tn), lambda i, j: (i, j)),
        out_shape=jax.ShapeDtypeStruct((m, n), out_dtype),
        compiler_params=_params("arbitrary", "arbitrary"),
        name=name,
    )(a, w)


def _mm_acc_kernel(a_ref, w_ref, o_ref, acc_ref):
    kk = pl.program_id(2)

    @pl.when(kk == 0)
    def _():
        acc_ref[...] = jnp.zeros_like(acc_ref)

    acc_ref[...] += _dot(a_ref[...], w_ref[...])

    @pl.when(kk == pl.num_programs(2) - 1)
    def _():
        o_ref[...] = acc_ref[...].astype(o_ref.dtype)


def _mm_acc(a, w, out_dtype, tm, tn, tk, name="mm_acc"):
    m, k = a.shape
    n = w.shape[1]
    return pl.pallas_call(
        _mm_acc_kernel,
        grid=(m // tm, n // tn, k // tk),
        in_specs=[pl.BlockSpec((tm, tk), lambda i, j, kk: (i, kk)),
                  pl.BlockSpec((tk, tn), lambda i, j, kk: (kk, j))],
        out_specs=pl.BlockSpec((tm, tn), lambda i, j, kk: (i, j)),
        out_shape=jax.ShapeDtypeStruct((m, n), out_dtype),
        scratch_shapes=[pltpu.VMEM((tm, tn), F32)],
        compiler_params=_params("arbitrary", "arbitrary", "arbitrary"),
        name=name,
    )(a, w)


def _resid_prenorm_kernel(y_ref, x_ref, mod_ref, pw_ref, nw_ref, x1_ref, h_ref):
    x1 = x_ref[...] + mod_ref[2:3, :] * _rms(y_ref[...].astype(F32), pw_ref[...])
    x1_ref[...] = x1
    h = _rms(x1, nw_ref[...]) * (1.0 + mod_ref[4:5, :]) + mod_ref[3:4, :]
    h_ref[...] = h.astype(h_ref.dtype)


def _resid_prenorm(y, x2, mod3, post_w, pre_w, seq, tm):
    m, d = x2.shape
    row = pl.BlockSpec((tm, d), lambda i: (i, 0))
    vec = pl.BlockSpec((1, d), lambda i: (0, 0))
    return pl.pallas_call(
        _resid_prenorm_kernel,
        grid=(m // tm,),
        in_specs=[row, row, pl.BlockSpec((None, 6, d), lambda i: ((i * tm) // seq, 0, 0)), vec, vec],
        out_specs=[row, row],
        out_shape=[jax.ShapeDtypeStruct((m, d), F32), jax.ShapeDtypeStruct((m, d), BF16)],
        compiler_params=_params("arbitrary"),
        name="resid_prenorm",
    )(y, x2, mod3, post_w, pre_w)


def _resid_kernel(y_ref, x_ref, mod_ref, pw_ref, o_ref):
    o_ref[...] = x_ref[...] + mod_ref[5:6, :] * _rms(y_ref[...].astype(F32), pw_ref[...])


def _resid(y, x1, mod3, post_w, seq, tm):
    m, d = x1.shape
    row = pl.BlockSpec((tm, d), lambda i: (i, 0))
    return pl.pallas_call(
        _resid_kernel,
        grid=(m // tm,),
        in_specs=[row, row, pl.BlockSpec((None, 6, d), lambda i: ((i * tm) // seq, 0, 0)),
                  pl.BlockSpec((1, d), lambda i: (0, 0))],
        out_specs=row,
        out_shape=jax.ShapeDtypeStruct((m, d), F32),
        compiler_params=_params("arbitrary"),
        name="resid",
    )(y, x1, mod3, post_w)


def _tile(n, pref):
    t = min(n, pref)
    while n % t:
        t //= 2
    return t


def _layer(x2, mod3, batch, seq, mix_pre_norm, mix_post_norm, w_in, lru_conv_w, lru_conv_b,
           lru_gate_a_w, lru_gate_a_b, lru_gate_i_w, lru_gate_i_b, lru_lambda, gdn_conv_w,
           gdn_a_log, gdn_dt_bias, gdn_out_norm, w_branch_lru, w_branch_gdn, w_out,
           mlp_pre_norm, mlp_post_norm, w_mlp_up, w_mlp_down):
    m, d = x2.shape
    lw = lru_lambda.shape[-1]
    nblk = lru_gate_a_w.shape[0]
    heads = gdn_a_log.shape[-1]
    dv = gdn_out_norm.shape[-1]
    val = heads * dv
    conv_dim = gdn_conv_w.shape[-1]
    dk = (conv_dim - val) // (2 * heads)
    lanes = V7X_LANES

    o_a = 2 * lw + conv_dim + val
    gate_off = o_a
    n_main = o_a + 2 * d
    tn_in = _tile(o_a, 1024)
    w_main = _wprep(w_in, n_main, o_a, 2 * heads, _tile(d, 1024), tn_in)
    w_ab = jnp.pad(w_in[:, o_a:o_a + 2 * heads], ((0, 0), (0, lanes - 2 * heads))).astype(BF16)

    tm = _tile(m, 512)
    proj, ab = _inproj(x2, mod3, mix_pre_norm[None, :], w_main, w_ab, seq, tm, tn_in)
    proj3 = proj.reshape(batch, seq, n_main)

    groups = 4 if nblk % 4 == 0 else 1
    w_ai = jnp.concatenate([lru_gate_a_w, lru_gate_i_w], axis=-1).astype(BF16)
    b_ai = jnp.concatenate([lru_gate_a_b, lru_gate_i_b], axis=-1)[:, None, :]
    ya = _lru(proj3, lru_conv_w, lru_conv_b[None, :], w_ai, b_ai, lru_lambda[None, :], lw, groups,
              _tile(seq, 512))

    pad_h = lambda v: jnp.pad(v, (0, lanes - heads))[None, :]
    gcol, grow = _gdn_prep(ab.reshape(batch, seq, lanes), pad_h(gdn_a_log), pad_h(gdn_dt_bias),
                           heads, GDN_CHUNK)
    nh = GDN_HEADS_PER_STEP if heads % GDN_HEADS_PER_STEP == 0 else 1
    yb = _gdn(proj3, gdn_conv_w, gcol, grow, gdn_out_norm[None, :], heads, dk, dv, 2 * lw, GDN_CHUNK,
              nh, _tile(seq, GDN_ROWS_PER_STEP))

    merged = _merge(ya.reshape(m, lw), yb.reshape(m, val), w_branch_lru.astype(BF16),
                    w_branch_gdn.astype(BF16), proj, gate_off, tm, _tile(d, 1024))
    tm_big = _tile(m, 1024)
    y1 = _mm(merged, w_out.astype(BF16), BF16, tm_big, _tile(d, 1024), name="out_proj")
    x1, h2 = _resid_prenorm(y1, x2, mod3, mix_post_norm[None, :], mlp_pre_norm[None, :], seq, _tile(m, 256))

    d_ff = w_mlp_up.shape[1]
    hid = _mm(h2, w_mlp_up.astype(BF16), BF16, tm_big, _tile(d_ff, 1024), relu2=True, name="mlp_up")
    y2 = _mm_acc(hid, w_mlp_down.astype(BF16), BF16, tm_big, _tile(d, 2048), _tile(d_ff, 2048), name="mlp_down")
    return _resid(y2, x1, mod3, mlp_post_norm[None, :], seq, _tile(m, 256))


def kernel(x, c, w_ada, b_ada, mix_pre_norm, mix_post_norm, w_in, lru_conv_w, lru_conv_b, lru_gate_a_w, lru_gate_a_b, lru_gate_i_w, lru_gate_i_b, lru_lambda, gdn_conv_w, gdn_a_log, gdn_dt_bias, gdn_out_norm, w_branch_lru, w_branch_gdn, w_out, mlp_pre_norm, mlp_post_norm, w_mlp_up, w_mlp_down):
    batch, seq, d = x.shape
    depth = w_ada.shape[0]
    x2 = x.reshape(batch * seq, d)
    c_pad = jnp.pad(c, ((0, V7X_SUBLANES - batch % V7X_SUBLANES), (0, 0))) if batch % V7X_SUBLANES else c
    for l in range(depth):
        mod = _adaln(c_pad, w_ada[l], b_ada[l][None, :], _tile(6 * d, 512))
        mod3 = mod[:batch].reshape(batch, 6, d)
        x2 = _layer(x2, mod3, batch, seq, mix_pre_norm[l], mix_post_norm[l], w_in[l], lru_conv_w[l],
                    lru_conv_b[l], lru_gate_a_w[l], lru_gate_a_b[l], lru_gate_i_w[l], lru_gate_i_b[l],
                    lru_lambda[l], gdn_conv_w[l], gdn_a_log[l], gdn_dt_bias[l], gdn_out_norm[l],
                    w_branch_lru[l], w_branch_gdn[l], w_out[l], mlp_pre_norm[l], mlp_post_norm[l],
                    w_mlp_up[l], w_mlp_down[l])
    return x2.reshape(batch, seq, d)
```

```python
import functools

import jax
import jax.numpy as jnp
from jax import lax
from jax.experimental import pallas as pl
from jax.experimental.pallas import tpu as pltpu

F32 = jnp.float32
BF16 = jnp.bfloat16

EPS = 1e-6
CONV_W = 4
LRU_C = 8.0
V7X_LANES = 128
V7X_SUBLANES = 8
V7X_MXU_COLS = 256
V7X_VMEM_LIMIT_BYTES = 56 * 1024 * 1024
GDN_CHUNK = 128
GDN_HEADS_PER_STEP = 4
GDN_ROWS_PER_STEP = 512
LRU_ROWS_PER_STEP = 512
LRU_SEG_PAD_ROWS = 8


def _params(*sem):
    return pltpu.CompilerParams(dimension_semantics=sem, vmem_limit_bytes=V7X_VMEM_LIMIT_BYTES)


def _sigmoid(x):
    return 1.0 / (1.0 + jnp.exp(-x))


def _silu(x):
    return x * _sigmoid(x)


def _softplus(x):
    return jnp.maximum(x, 0.0) + jnp.log1p(jnp.exp(-jnp.abs(x)))


def _gelu_tanh(x):
    c = 0.7978845608028654
    return 0.5 * x * (1.0 + jnp.tanh(c * (x + 0.044715 * (x * x * x))))


def _rms(x, w):
    return x * lax.rsqrt(jnp.mean(x * x, axis=-1, keepdims=True) + EPS) * w


def _causal_conv(prev8, cur, w):
    rows = cur.shape[0]
    xw = jnp.concatenate([prev8, cur], axis=0)
    out = w[3:4, :] * cur
    for back in (1, 2, 3):
        out = out + w[3 - back:4 - back, :] * pltpu.roll(xw, back, 0)[V7X_SUBLANES:V7X_SUBLANES + rows, :]
    return out


def _dot(a, b):
    return jnp.dot(a, b, preferred_element_type=F32)


def _dot_split(a, b):
    a_hi = a.astype(BF16)
    b_hi = b.astype(BF16)
    a_lo = (a - a_hi.astype(F32)).astype(BF16)
    b_lo = (b - b_hi.astype(F32)).astype(BF16)
    return (_dot(a_lo, b_hi) + _dot(a_hi, b_lo)) + _dot(a_hi, b_hi)


def _dot_nt(a, b):
    return lax.dot_general(a, b, (((1,), (1,)), ((), ())), preferred_element_type=F32)


def _adaln_kernel(c_ref, w_ref, b_ref, o_ref):
    ca = _silu(c_ref[...])
    o_ref[...] = _dot(ca.astype(BF16), w_ref[...].astype(BF16)) + b_ref[...]


def _adaln(c_pad, w_ada, b_ada, tn):
    rows, d = c_pad.shape
    n = w_ada.shape[1]
    return pl.pallas_call(
        _adaln_kernel,
        grid=(n // tn,),
        in_specs=[pl.BlockSpec((rows, d), lambda j: (0, 0)),
                  pl.BlockSpec((d, tn), lambda j: (0, j)),
                  pl.BlockSpec((1, tn), lambda j: (0, j))],
        out_specs=pl.BlockSpec((rows, tn), lambda j: (0, j)),
        out_shape=jax.ShapeDtypeStruct((rows, n), F32),
        compiler_params=_params("arbitrary"),
        name="adaln",
    )(c_pad, w_ada, b_ada)


def _wprep_kernel(a_ref, b_ref, o_ref, *, n_aligned, skip):
    j = pl.program_id(1)

    @pl.when(j < n_aligned)
    def _():
        o_ref[...] = jnp.transpose(a_ref[...]).astype(o_ref.dtype)

    @pl.when(j >= n_aligned)
    def _():
        src = jnp.concatenate([a_ref[skip:, :], b_ref[...]], axis=0)
        o_ref[...] = jnp.transpose(src).astype(o_ref.dtype)


def _wprep(w_in_t, n_out, n_aligned_rows, skip, tk, tn):
    n_src, k = w_in_t.shape
    assert n_aligned_rows % tn == 0 and n_out % tn == 0 and tn % skip == 0 and skip % V7X_SUBLANES == 0
    assert n_src == n_out + skip
    kern = functools.partial(_wprep_kernel, n_aligned=n_aligned_rows // tn, skip=skip)
    return pl.pallas_call(
        kern,
        grid=(k // tk, n_out // tn),
        in_specs=[pl.BlockSpec((tn, tk), lambda i, j: (j, i)),
                  pl.BlockSpec((skip, tk), lambda i, j: ((j + 1) * (tn // skip), i))],
        out_specs=pl.BlockSpec((tk, tn), lambda i, j: (i, j)),
        out_shape=jax.ShapeDtypeStruct((k, n_out), BF16),
        compiler_params=_params("arbitrary", "arbitrary"),
        name="wprep",
    )(w_in_t, w_in_t)


def _prenorm_kernel(x_ref, mod_ref, nw_ref, wab_ref, h_ref, ab_ref):
    h = _rms(x_ref[...], nw_ref[...]) * (1.0 + mod_ref[1:2, :]) + mod_ref[0:1, :]
    hb = h.astype(BF16)
    h_ref[...] = hb
    ab_ref[...] = _dot(hb, wab_ref[...])


def _prenorm(x2, mod3, norm_w, w_ab, seq, tm):
    m, d = x2.shape
    nab = w_ab.shape[1]
    return pl.pallas_call(
        _prenorm_kernel,
        grid=(m // tm,),
        in_specs=[pl.BlockSpec((tm, d), lambda i: (i, 0)),
                  pl.BlockSpec((None, 6, d), lambda i: ((i * tm) // seq, 0, 0)),
                  pl.BlockSpec((1, d), lambda i: (0, 0)),
                  pl.BlockSpec((d, nab), lambda i: (0, 0))],
        out_specs=[pl.BlockSpec((tm, d), lambda i: (i, 0)),
                   pl.BlockSpec((tm, nab), lambda i: (i, 0))],
        out_shape=[jax.ShapeDtypeStruct((m, d), BF16),
                   jax.ShapeDtypeStruct((m, nab), F32)],
        compiler_params=_params("arbitrary"),
        name="prenorm",
    )(x2, mod3, norm_w, w_ab)


def _proj_kernel(*refs, act, conv, l2_scale, seq, group):
    if conv:
        h_ref, w_ref, cw_ref, cb_ref, o_ref, tail_scr = refs
    else:
        h_ref, w_ref, o_ref = refs
    tm = h_ref.shape[0]
    tn = w_ref.shape[1]
    panel = min(tn, V7X_MXU_COLS)
    j = pl.program_id(1)
    if conv:
        new_seq = (pl.program_id(0) * tm) % seq == 0
    y_next = _dot(h_ref[...], w_ref[:, 0:panel])
    for p0 in range(0, tn, panel):
        cols = slice(p0, p0 + panel)
        y = y_next
        if p0 + panel < tn:
            y_next = _dot(h_ref[...], w_ref[:, p0 + panel:p0 + 2 * panel])
        if conv:
            prev = jnp.where(new_seq, 0.0, tail_scr[j, :, cols])
            tail_scr[j, :, cols] = y[tm - V7X_SUBLANES:, :]
            y = _causal_conv(prev, y, cw_ref[:, cols]) + cb_ref[:, cols]
        if act == "silu":
            y = _silu(y)
        elif act == "gelu":
            y = _gelu_tanh(y)
        elif act == "sigmoid":
            y = _sigmoid(y)
        if l2_scale is None:
            o_ref[:, cols] = y.astype(o_ref.dtype)
        else:
            for g0 in range(0, panel, group):
                yg = y[:, g0:g0 + group]
                yg = yg * (lax.rsqrt(jnp.sum(yg * yg, axis=-1, keepdims=True) + EPS) * l2_scale)
                o_ref[:, p0 + g0:p0 + g0 + group] = yg.astype(o_ref.dtype)


def _proj(h, w_main, col_base, n_cols, seq, tm, tn, act=None, conv_w=None, conv_b=None, l2_scale=None,
          group=V7X_LANES, name="proj"):
    m, d = h.shape
    assert col_base % tn == 0 and n_cols % tn == 0 and seq % tm == 0
    jb = col_base // tn
    conv = conv_w is not None
    kern = functools.partial(_proj_kernel, act=act, conv=conv, l2_scale=l2_scale, seq=seq, group=group)
    in_specs = [pl.BlockSpec((tm, d), lambda i, j: (i, 0)),
                pl.BlockSpec((d, tn), lambda i, j: (0, jb + j))]
    args = [h, w_main]
    scratch = []
    if conv:
        if conv_b is None:
            conv_b = jnp.zeros((1, n_cols), F32)
        in_specs += [pl.BlockSpec((CONV_W, tn), lambda i, j: (0, j)),
                     pl.BlockSpec((1, tn), lambda i, j: (0, j))]
        args += [conv_w, conv_b]
        scratch = [pltpu.VMEM((n_cols // tn, V7X_SUBLANES, tn), F32)]
    return pl.pallas_call(
        kern,
        grid=(m // tm, n_cols // tn),
        in_specs=in_specs,
        out_specs=pl.BlockSpec((tm, tn), lambda i, j: (i, j)),
        out_shape=jax.ShapeDtypeStruct((m, n_cols), BF16),
        scratch_shapes=scratch,
        compiler_params=_params("arbitrary", "arbitrary"),
        name=name,
    )(*args)


def _lru_kernel(x_ref, gate_ref, wai_ref, bai_ref, lam_ref, o_ref, a_scr, u_scr, h_scr, *, groups, tb):
    blk = V7X_LANES
    nseg = V7X_SUBLANES
    seg = tb // nseg
    pitch = seg + LRU_SEG_PAD_ROWS

    @pl.when(pl.program_id(2) == 0)
    def _():
        h_scr[...] = jnp.zeros_like(h_scr)

    sp = _softplus(-lam_ref[...])
    for g in range(groups):
        sl = slice(g * blk, (g + 1) * blk)
        xg = x_ref[:, sl].astype(F32)
        ri = _dot(x_ref[:, sl], wai_ref[g]) + bai_ref[g]
        r = _sigmoid(ri[:, :blk])
        i = _sigmoid(ri[:, blk:])
        log_a = (-LRU_C) * r * sp[:, sl]
        a = jnp.exp(log_a)
        u = jnp.sqrt(-jnp.tanh(log_a) * (1.0 + a * a)) * (i * xg)
        for s in range(nseg):
            a_scr[g, s * pitch:s * pitch + seg, :] = a[s * seg:(s + 1) * seg, :]
            u_scr[g, s * pitch:s * pitch + seg, :] = u[s * seg:(s + 1) * seg, :]

    def step(i, carry):
        out = []
        for g in range(groups):
            p, h = carry[g]
            idx = pl.ds(i, nseg, stride=pitch)
            a = a_scr[g, idx, :]
            u = u_scr[g, idx, :]
            p = a * p
            h = a * h + u
            a_scr[g, idx, :] = p
            u_scr[g, idx, :] = h
            out.append((p, h))
        return tuple(out)

    init = tuple((jnp.ones((nseg, blk), F32), jnp.zeros((nseg, blk), F32)) for _ in range(groups))
    ends = lax.fori_loop(0, seg, step, init, unroll=8)

    for g in range(groups):
        sl = slice(g * blk, (g + 1) * blk)
        p_end, h_end = ends[g]
        carry = h_scr[0:1, sl]
        for s in range(nseg):
            rows = slice(s * pitch, s * pitch + seg)
            h = u_scr[g, rows, :] + a_scr[g, rows, :] * carry
            o_ref[s * seg:(s + 1) * seg, sl] = (h * gate_ref[s * seg:(s + 1) * seg, sl].astype(F32)).astype(o_ref.dtype)
            carry = p_end[s:s + 1, :] * carry + h_end[s:s + 1, :]
        h_scr[0:1, sl] = carry


def _lru(xa3, gate3, w_ai, b_ai, lam, groups, tb):
    b, t, lw = xa3.shape
    blk = V7X_LANES
    w = groups * blk
    pitch = tb // V7X_SUBLANES + LRU_SEG_PAD_ROWS
    kern = functools.partial(_lru_kernel, groups=groups, tb=tb)
    blk_spec = pl.BlockSpec((None, tb, w), lambda bi, ci, ti: (bi, ti, ci))
    seg_scr = pltpu.VMEM((groups, V7X_SUBLANES * pitch, blk), F32)
    return pl.pallas_call(
        kern,
        grid=(b, lw // w, t // tb),
        in_specs=[blk_spec, blk_spec,
                  pl.BlockSpec((groups, blk, 2 * blk), lambda bi, ci, ti: (ci, 0, 0)),
                  pl.BlockSpec((groups, 1, 2 * blk), lambda bi, ci, ti: (ci, 0, 0)),
                  pl.BlockSpec((1, w), lambda bi, ci, ti: (0, ci))],
        out_specs=blk_spec,
        out_shape=jax.ShapeDtypeStruct((b, t, lw), BF16),
        scratch_shapes=[seg_scr, seg_scr, pltpu.VMEM((V7X_SUBLANES, w), F32)],
        compiler_params=_params("arbitrary", "arbitrary", "arbitrary"),
        name="rglru",
    )(xa3, gate3, w_ai, b_ai, lam)


def _gdn_prep_kernel(ab_ref, alog_ref, dtb_ref, gcol_ref, grow_ref, *, heads, chunk):
    t = ab_ref.shape[0]
    ab = ab_ref[...]
    g = -jnp.exp(alog_ref[...]) * _softplus(ab + dtb_ref[...])
    lane = lax.broadcasted_iota(jnp.int32, ab.shape, 1)
    row = lax.broadcasted_iota(jnp.int32, ab.shape, 0)
    pos = row & (chunk - 1)
    s = 1
    while s < chunk:
        g = g + jnp.where(pos >= s, pltpu.roll(g, s, 0), 0.0)
        s *= 2
    gcol_ref[...] = jnp.where(lane < heads, g, _sigmoid(ab))
    for c in range(t // chunk):
        grow_ref[c] = jnp.transpose(g[c * chunk:(c + 1) * chunk, :])[0:2 * V7X_SUBLANES, :]


def _gdn_prep(ab3, alog_pad, dtb_pad, heads, chunk):
    b, t, lanes = ab3.shape
    assert chunk == lanes and heads <= 2 * V7X_SUBLANES
    kern = functools.partial(_gdn_prep_kernel, heads=heads, chunk=chunk)
    return pl.pallas_call(
        kern,
        grid=(b,),
        in_specs=[pl.BlockSpec((None, t, lanes), lambda bi: (bi, 0, 0)),
                  pl.BlockSpec((1, lanes), lambda bi: (0, 0)),
                  pl.BlockSpec((1, lanes), lambda bi: (0, 0))],
        out_specs=[pl.BlockSpec((None, t, lanes), lambda bi: (bi, 0, 0)),
                   pl.BlockSpec((None, t // chunk, 2 * V7X_SUBLANES, chunk), lambda bi: (bi, 0, 0, 0))],
        out_shape=[jax.ShapeDtypeStruct((b, t, lanes), F32),
                   jax.ShapeDtypeStruct((b, t // chunk, 2 * V7X_SUBLANES, chunk), F32)],
        compiler_params=_params("arbitrary"),
        name="gdn_prep",
    )(ab3, alog_pad, dtb_pad)


def _gdn_kernel(q_ref, k_ref, v_ref, z_ref, gcol_ref, grow_ref, nw_ref, o_ref, s_scr,
                *, heads, nh, chunk, dk):
    c_len = chunk
    tb = q_ref.shape[0]
    n_chunks = tb // c_len
    head0 = pl.program_id(1) * nh

    @pl.when(pl.program_id(2) == 0)
    def _():
        s_scr[...] = jnp.zeros_like(s_scr)

    lane = lax.broadcasted_iota(jnp.int32, (c_len, V7X_LANES), 1)
    ri = lax.broadcasted_iota(jnp.int32, (c_len, c_len), 0)
    ci = lax.broadcasted_iota(jnp.int32, (c_len, c_len), 1)

    pairs = [(c, h) for c in range(n_chunks) for h in range(nh)]
    loc = {}
    for c, h in pairs:
        rows = slice(c * c_len, (c + 1) * c_len)
        cols = slice(h * dk, (h + 1) * dk)
        head = head0 + h
        gb = gcol_ref[rows, :]
        k_bf = k_ref[rows, cols]
        q = q_ref[rows, cols].astype(F32)
        k = k_bf.astype(F32)
        g_col = jnp.sum(jnp.where(lane == head, gb, 0.0), axis=-1, keepdims=True)
        b_col = jnp.sum(jnp.where(lane == head + heads, gb, 0.0), axis=-1, keepdims=True)
        g_row = grow_ref[c, pl.ds(head, 1), :]
        e_g = jnp.exp(g_col)
        g_last = g_col[c_len - 1:c_len, :]
        kb = k * b_col
        loc[c, h] = dict(
            decay=jnp.exp(jnp.where(ri >= ci, g_col - g_row, -jnp.inf)),
            kq_lhs=jnp.concatenate([kb.astype(BF16), q_ref[rows, cols]], axis=0), k_bf=k_bf,
            rhs=jnp.concatenate([v_ref[rows, cols].astype(F32) * b_col, kb * e_g], axis=1),
            qe=q * e_g, kd=k * jnp.exp(g_last - g_col), e_last=jnp.exp(g_last))

    for p in pairs:
        loc[p]["kq"] = _dot_nt(loc[p]["kq_lhs"], loc[p]["k_bf"])
    for p in pairs:
        d = loc[p]
        d["x"] = jnp.where(ri > ci, -(d["kq"][:c_len] * d["decay"]), 0.0)
        d["attn"] = d["kq"][c_len:] * d["decay"]
        d["pow"] = d["x"]
        d["t_off"] = d["x"]
    span = 2
    while span < c_len:
        for p in pairs:
            xb = loc[p]["pow"].astype(BF16)
            loc[p]["pow"] = _dot(xb, xb)
        for p in pairs:
            d = loc[p]
            d["t_off"] = d["t_off"] + d["pow"] + _dot(d["t_off"].astype(BF16), d["pow"].astype(BF16))
        span *= 2
    for p in pairs:
        d = loc[p]
        d["t_bf"] = d["t_off"].astype(BF16)
        d["sol"] = d["rhs"] + _dot(d["t_bf"], d["rhs"].astype(BF16))
    for p in pairs:
        d = loc[p]
        d["res"] = (d["rhs"] - d["sol"]) + _dot_split(d["x"], d["sol"])
    for p in pairs:
        d = loc[p]
        sol = d["sol"] + (d["res"] + _dot(d["t_bf"], d["res"].astype(BF16)))
        d["u"] = sol[:, :dk]
        d["lhs_s"] = jnp.concatenate([sol[:, dk:], d["qe"]], axis=0).astype(BF16)
        d["lhs_v"] = jnp.concatenate([d["attn"], jnp.transpose(d["kd"])], axis=0).astype(BF16)

    state = [s_scr[h] for h in range(nh)]
    for c in range(n_chunks):
        rows = slice(c * c_len, (c + 1) * c_len)
        r_s = [_dot(loc[c, h]["lhs_s"], state[h].astype(BF16)) for h in range(nh)]
        v_new = [loc[c, h]["u"] - r_s[h][:c_len] for h in range(nh)]
        r_v = [_dot(loc[c, h]["lhs_v"], v_new[h].astype(BF16)) for h in range(nh)]
        for h in range(nh):
            cols = slice(h * dk, (h + 1) * dk)
            state[h] = state[h] * loc[c, h]["e_last"] + r_v[h][c_len:]
            o = r_s[h][c_len:] + r_v[h][:c_len]
            o = _rms(o, nw_ref[...]) * z_ref[rows, cols].astype(F32)
            o_ref[rows, cols] = o.astype(o_ref.dtype)
    for h in range(nh):
        s_scr[h] = state[h]


def _gdn(q3, k3, v3, z3, gcol, grow, norm_w, heads, dk, dv, chunk, nh, tb):
    b, t, _ = q3.shape
    assert dk == V7X_LANES and dv == V7X_LANES and heads % nh == 0 and t % tb == 0 and tb % chunk == 0
    w = nh * dk
    kern = functools.partial(_gdn_kernel, heads=heads, nh=nh, chunk=chunk, dk=dk)
    col = pl.BlockSpec((None, tb, w), lambda bi, hi, ti: (bi, ti, hi))
    return pl.pallas_call(
        kern,
        grid=(b, heads // nh, t // tb),
        in_specs=[col, col, col, col,
                  pl.BlockSpec((None, tb, V7X_LANES), lambda bi, hi, ti: (bi, ti, 0)),
                  pl.BlockSpec((None, tb // chunk, 2 * V7X_SUBLANES, chunk), lambda bi, hi, ti: (bi, ti, 0, 0)),
                  pl.BlockSpec((1, dv), lambda bi, hi, ti: (0, 0))],
        out_specs=col,
        out_shape=jax.ShapeDtypeStruct((b, t, heads * dv), BF16),
        scratch_shapes=[pltpu.VMEM((nh, dk, dv), F32)],
        compiler_params=_params("arbitrary", "arbitrary", "arbitrary"),
        name="gdn",
    )(q3, k3, v3, z3, gcol, grow, norm_w)


def _merge_kernel(ya_ref, yb_ref, wl_ref, wg_ref, gl_ref, gg_ref, o_ref):
    pa = _dot(ya_ref[...], wl_ref[...])
    pb = _dot(yb_ref[...], wg_ref[...])
    o_ref[...] = (gl_ref[...].astype(F32) * pa + gg_ref[...].astype(F32) * pb).astype(o_ref.dtype)


def _merge(ya2, yb2, w_l, w_g, gates, tm, tn):
    m, ka = ya2.shape
    kb = yb2.shape[1]
    d = w_l.shape[1]
    assert d % tn == 0
    gg0 = d // tn
    return pl.pallas_call(
        _merge_kernel,
        grid=(m // tm, d // tn),
        in_specs=[pl.BlockSpec((tm, ka), lambda i, j: (i, 0)),
                  pl.BlockSpec((tm, kb), lambda i, j: (i, 0)),
                  pl.BlockSpec((ka, tn), lambda i, j: (0, j)),
                  pl.BlockSpec((kb, tn), lambda i, j: (0, j)),
                  pl.BlockSpec((tm, tn), lambda i, j: (i, j)),
                  pl.BlockSpec((tm, tn), lambda i, j: (i, gg0 + j))],
        out_specs=pl.BlockSpec((tm, tn), lambda i, j: (i, j)),
        out_shape=jax.ShapeDtypeStruct((m, d), BF16),
        compiler_params=_params("arbitrary", "arbitrary"),
        name="merge",
    )(ya2, yb2, w_l, w_g, gates, gates)


def _mm_kernel(a_ref, w_ref, o_ref, *, relu2):
    y = _dot(a_ref[...], w_ref[...])
    if relu2:
        y = jnp.square(jnp.maximum(y, 0.0))
    o_ref[...] = y.astype(o_ref.dtype)


def _mm(a, w, out_dtype, tm, tn, relu2=False, name="mm"):
    m, k = a.shape
    n = w.shape[1]
    return pl.pallas_call(
        functools.partial(_mm_kernel, relu2=relu2),
        grid=(m // tm, n // tn),
        in_specs=[pl.BlockSpec((tm, k), lambda i, j: (i, 0)),
                  pl.BlockSpec((k, tn), lambda i, j: (0, j))],
        out_specs=pl.BlockSpec((tm, tn), lambda i, j: (i, j)),
        out_shape=jax.ShapeDtypeStruct((m, n), out_dtype),
        compiler_params=_params("arbitrary", "arbitrary"),
        name=name,
    )(a, w)


def _mm_acc_kernel(a_ref, w_ref, o_ref, acc_ref):
    kk = pl.program_id(2)

    @pl.when(kk == 0)
    def _():
        acc_ref[...] = jnp.zeros_like(acc_ref)

    acc_ref[...] += _dot(a_ref[...], w_ref[...])

    @pl.when(kk == pl.num_programs(2) - 1)
    def _():
        o_ref[...] = acc_ref[...].astype(o_ref.dtype)


def _mm_acc(a, w, out_dtype, tm, tn, tk, name="mm_acc"):
    m, k = a.shape
    n = w.shape[1]
    return pl.pallas_call(
        _mm_acc_kernel,
        grid=(m // tm, n // tn, k // tk),
        in_specs=[pl.BlockSpec((tm, tk), lambda i, j, kk: (i, kk)),
                  pl.BlockSpec((tk, tn), lambda i, j, kk: (kk, j))],
        out_specs=pl.BlockSpec((tm, tn), lambda i, j, kk: (i, j)),
        out_shape=jax.ShapeDtypeStruct((m, n), out_dtype),
        scratch_shapes=[pltpu.VMEM((tm, tn), F32)],
        compiler_params=_params("arbitrary", "arbitrary", "arbitrary"),
        name=name,
    )(a, w)


def _resid_prenorm_kernel(y_ref, x_ref, mod_ref, pw_ref, nw_ref, x1_ref, h_ref):
    x1 = x_ref[...] + mod_ref[2:3, :] * _rms(y_ref[...].astype(F32), pw_ref[...])
    x1_ref[...] = x1
    h = _rms(x1, nw_ref[...]) * (1.0 + mod_ref[4:5, :]) + mod_ref[3:4, :]
    h_ref[...] = h.astype(h_ref.dtype)


def _resid_prenorm(y, x2, mod3, post_w, pre_w, seq, tm):
    m, d = x2.shape
    row = pl.BlockSpec((tm, d), lambda i: (i, 0))
    vec = pl.BlockSpec((1, d), lambda i: (0, 0))
    return pl.pallas_call(
        _resid_prenorm_kernel,
        grid=(m // tm,),
        in_specs=[row, row, pl.BlockSpec((None, 6, d), lambda i: ((i * tm) // seq, 0, 0)), vec, vec],
        out_specs=[row, row],
        out_shape=[jax.ShapeDtypeStruct((m, d), F32), jax.ShapeDtypeStruct((m, d), BF16)],
        compiler_params=_params("arbitrary"),
        name="resid_prenorm",
    )(y, x2, mod3, post_w, pre_w)


def _resid_kernel(y_ref, x_ref, mod_ref, pw_ref, o_ref):
    o_ref[...] = x_ref[...] + mod_ref[5:6, :] * _rms(y_ref[...].astype(F32), pw_ref[...])


def _resid(y, x1, mod3, post_w, seq, tm):
    m, d = x1.shape
    row = pl.BlockSpec((tm, d), lambda i: (i, 0))
    return pl.pallas_call(
        _resid_kernel,
        grid=(m // tm,),
        in_specs=[row, row, pl.BlockSpec((None, 6, d), lambda i: ((i * tm) // seq, 0, 0)),
                  pl.BlockSpec((1, d), lambda i: (0, 0))],
        out_specs=row,
        out_shape=jax.ShapeDtypeStruct((m, d), F32),
        compiler_params=_params("arbitrary"),
        name="resid",
    )(y, x1, mod3, post_w)


def _tile(n, pref):
    t = min(n, pref)
    while n % t:
        t //= 2
    return t


def _layer(x2, mod3, batch, seq, mix_pre_norm, mix_post_norm, w_in, lru_conv_w, lru_conv_b,
           lru_gate_a_w, lru_gate_a_b, lru_gate_i_w, lru_gate_i_b, lru_lambda, gdn_conv_w,
           gdn_a_log, gdn_dt_bias, gdn_out_norm, w_branch_lru, w_branch_gdn, w_out,
           mlp_pre_norm, mlp_post_norm, w_mlp_up, w_mlp_down):
    m, d = x2.shape
    lw = lru_lambda.shape[-1]
    nblk = lru_gate_a_w.shape[0]
    heads = gdn_a_log.shape[-1]
    dv = gdn_out_norm.shape[-1]
    val = heads * dv
    conv_dim = gdn_conv_w.shape[-1]
    key = (conv_dim - val) // 2
    dk = key // heads
    lanes = V7X_LANES

    o_q = 2 * lw
    o_z = o_q + conv_dim
    o_a = o_z + val
    n_main = o_a + 2 * d
    w_in_t = jnp.swapaxes(w_in, 0, 1)
    w_main = _wprep(w_in_t, n_main, o_a, 2 * heads, _tile(d, 1024), _tile(o_a, 1024))
    w_ab = jnp.pad(w_in[:, o_a:o_a + 2 * heads], ((0, 0), (0, lanes - 2 * heads))).astype(BF16)

    h, ab = _prenorm(x2, mod3, mix_pre_norm[None, :], w_ab, seq, _tile(m, 256))

    tm = _tile(seq, 1024)
    tn = _tile(min(lw, key, val), 1024)
    proj = functools.partial(_proj, h, w_main, seq=seq, tm=tm, tn=tn)
    xa = proj(0, lw, conv_w=lru_conv_w, conv_b=lru_conv_b[None, :], name="proj_lru_x")
    gel = proj(lw, lw, act="gelu", name="proj_lru_gate")
    gq = proj(o_q, key, act="silu", conv_w=gdn_conv_w[:, :key], l2_scale=dk ** -0.5, group=dk, name="proj_q")
    gk = proj(o_q + key, key, act="silu", conv_w=gdn_conv_w[:, key:2 * key], l2_scale=1.0, group=dk, name="proj_k")
    gv = proj(o_q + 2 * key, val, act="silu", conv_w=gdn_conv_w[:, 2 * key:], name="proj_v")
    gz = proj(o_z, val, act="silu", name="proj_z")
    gates = proj(o_a, 2 * d, act="sigmoid", name="proj_gates")

    groups = 4 if nblk % 4 == 0 else 1
    w_ai = jnp.concatenate([lru_gate_a_w, lru_gate_i_w], axis=-1).astype(BF16)
    b_ai = jnp.concatenate([lru_gate_a_b, lru_gate_i_b], axis=-1)[:, None, :]
    as3 = lambda a: a.reshape(batch, seq, a.shape[-1])
    ya = _lru(as3(xa), as3(gel), w_ai, b_ai, lru_lambda[None, :], groups, _tile(seq, LRU_ROWS_PER_STEP))

    pad_h = lambda v: jnp.pad(v, (0, lanes - heads))[None, :]
    gcol, grow = _gdn_prep(as3(ab), pad_h(gdn_a_log), pad_h(gdn_dt_bias), heads, GDN_CHUNK)
    nh = GDN_HEADS_PER_STEP if heads % GDN_HEADS_PER_STEP == 0 else 1
    yb = _gdn(as3(gq), as3(gk), as3(gv), as3(gz), gcol, grow, gdn_out_norm[None, :], heads, dk, dv,
              GDN_CHUNK, nh, _tile(seq, GDN_ROWS_PER_STEP))

    tm_mid = _tile(m, 512)
    merged = _merge(ya.reshape(m, lw), yb.reshape(m, val), w_branch_lru.astype(BF16),
                    w_branch_gdn.astype(BF16), gates, tm_mid, _tile(d, 1024))
    tm_big = _tile(m, 1024)
    y1 = _mm(merged, w_out.astype(BF16), BF16, tm_big, _tile(d, 1024), name="out_proj")
    x1, h2 = _resid_prenorm(y1, x2, mod3, mix_post_norm[None, :], mlp_pre_norm[None, :], seq, _tile(m, 256))

    d_ff = w_mlp_up.shape[1]
    hid = _mm(h2, w_mlp_up.astype(BF16), BF16, tm_big, _tile(d_ff, 1024), relu2=True, name="mlp_up")
    y2 = _mm_acc(hid, w_mlp_down.astype(BF16), BF16, tm_big, _tile(d, 2048), _tile(d_ff, 2048), name="mlp_down")
    return _resid(y2, x1, mod3, mlp_post_norm[None, :], seq, _tile(m, 256))


def kernel(x, c, w_ada, b_ada, mix_pre_norm, mix_post_norm, w_in, lru_conv_w, lru_conv_b, lru_gate_a_w, lru_gate_a_b, lru_gate_i_w, lru_gate_i_b, lru_lambda, gdn_conv_w, gdn_a_log, gdn_dt_bias, gdn_out_norm, w_branch_lru, w_branch_gdn, w_out, mlp_pre_norm, mlp_post_norm, w_mlp_up, w_mlp_down):
    batch, seq, d = x.shape
    depth = w_ada.shape[0]
    x2 = x.reshape(batch * seq, d)
    c_pad = jnp.pad(c, ((0, V7X_SUBLANES - batch % V7X_SUBLANES), (0, 0))) if batch % V7X_SUBLANES else c
    for l in range(depth):
        mod = _adaln(c_pad, w_ada[l], b_ada[l][None, :], _tile(6 * d, 512))
        mod3 = mod[:batch].reshape(batch, 6, d)
        x2 = _layer(x2, mod3, batch, seq, mix_pre_norm[l], mix_post_norm[l], w_in[l], lru_conv_w[l],
                    lru_conv_b[l], lru_gate_a_w[l], lru_gate_a_b[l], lru_gate_i_w[l], lru_gate_i_b[l],
                    lru_lambda[l], gdn_conv_w[l], gdn_a_log[l], gdn_dt_bias[l], gdn_out_norm[l],
                    w_branch_lru[l], w_branch_gdn[l], w_out[l], mlp_pre_norm[l], mlp_post_norm[l],
                    w_mlp_up[l], w_mlp_down[l])
    return x2.reshape(batch, seq, d)
```

```python
import functools

import jax
import jax.numpy as jnp
from jax import lax
from jax.experimental import pallas as pl
from jax.experimental.pallas import tpu as pltpu

F32 = jnp.float32
BF16 = jnp.bfloat16

EPS = 1e-6
CONV_W = 4
LRU_C = 8.0
V7X_LANES = 128
V7X_SUBLANES = 8
V7X_MXU_COLS = 256
V7X_VMEM_LIMIT_BYTES = 56 * 1024 * 1024
GDN_CHUNK = 128
GDN_HEADS_PER_STEP = 4
GDN_ROWS_PER_STEP = 512
LRU_ROWS_PER_STEP = 512
LRU_SEG_PAD_ROWS = 8


def _params(*sem):
    return pltpu.CompilerParams(dimension_semantics=sem, vmem_limit_bytes=V7X_VMEM_LIMIT_BYTES)


def _sigmoid(x):
    return 1.0 / (1.0 + jnp.exp(-x))


def _silu(x):
    return x * _sigmoid(x)


def _softplus(x):
    return jnp.maximum(x, 0.0) + jnp.log1p(jnp.exp(-jnp.abs(x)))


def _gelu_tanh(x):
    c = 0.7978845608028654
    return 0.5 * x * (1.0 + jnp.tanh(c * (x + 0.044715 * (x * x * x))))


def _rms(x, w):
    return x * lax.rsqrt(jnp.mean(x * x, axis=-1, keepdims=True) + EPS) * w


def _causal_conv(prev8, cur, w):
    rows = cur.shape[0]
    xw = jnp.concatenate([prev8, cur], axis=0)
    out = w[3:4, :] * cur
    for back in (1, 2, 3):
        out = out + w[3 - back:4 - back, :] * pltpu.roll(xw, back, 0)[V7X_SUBLANES:V7X_SUBLANES + rows, :]
    return out


def _dot(a, b):
    return jnp.dot(a, b, preferred_element_type=F32)


def _dot_split(a, b):
    a_hi = a.astype(BF16)
    b_hi = b.astype(BF16)
    a_lo = (a - a_hi.astype(F32)).astype(BF16)
    b_lo = (b - b_hi.astype(F32)).astype(BF16)
    stacked = _dot(jnp.concatenate([a_hi, a_lo], axis=1), jnp.concatenate([b_hi, b_hi], axis=0))
    return stacked + _dot(a_hi, b_lo)


def _dot_nt(a, b):
    return lax.dot_general(a, b, (((1,), (1,)), ((), ())), preferred_element_type=F32)


def _adaln_kernel(c_ref, w_ref, b_ref, o_ref):
    ca = _silu(c_ref[...])
    o_ref[...] = _dot(ca.astype(BF16), w_ref[...].astype(BF16)) + b_ref[...]


def _adaln(c_pad, w_ada, b_ada, tn):
    rows, d = c_pad.shape
    n = w_ada.shape[1]
    return pl.pallas_call(
        _adaln_kernel,
        grid=(n // tn,),
        in_specs=[pl.BlockSpec((rows, d), lambda j: (0, 0)),
                  pl.BlockSpec((d, tn), lambda j: (0, j)),
                  pl.BlockSpec((1, tn), lambda j: (0, j))],
        out_specs=pl.BlockSpec((rows, tn), lambda j: (0, j)),
        out_shape=jax.ShapeDtypeStruct((rows, n), F32),
        compiler_params=_params("arbitrary"),
        name="adaln",
    )(c_pad, w_ada, b_ada)


def _wprep_kernel(a_ref, b_ref, o_ref, *, n_aligned, skip):
    j = pl.program_id(1)

    @pl.when(j < n_aligned)
    def _():
        o_ref[...] = jnp.transpose(a_ref[...]).astype(o_ref.dtype)

    @pl.when(j >= n_aligned)
    def _():
        src = jnp.concatenate([a_ref[skip:, :], b_ref[...]], axis=0)
        o_ref[...] = jnp.transpose(src).astype(o_ref.dtype)


def _wprep(w_in_t, n_out, n_aligned_rows, skip, tk, tn):
    n_src, k = w_in_t.shape
    assert n_aligned_rows % tn == 0 and n_out % tn == 0 and tn % skip == 0 and skip % V7X_SUBLANES == 0
    assert n_src == n_out + skip
    kern = functools.partial(_wprep_kernel, n_aligned=n_aligned_rows // tn, skip=skip)
    return pl.pallas_call(
        kern,
        grid=(k // tk, n_out // tn),
        in_specs=[pl.BlockSpec((tn, tk), lambda i, j: (j, i)),
                  pl.BlockSpec((skip, tk), lambda i, j: ((j + 1) * (tn // skip), i))],
        out_specs=pl.BlockSpec((tk, tn), lambda i, j: (i, j)),
        out_shape=jax.ShapeDtypeStruct((k, n_out), BF16),
        compiler_params=_params("arbitrary", "arbitrary"),
        name="wprep",
    )(w_in_t, w_in_t)


def _prenorm_kernel(x_ref, mod_ref, nw_ref, wab_ref, h_ref, ab_ref):
    h = _rms(x_ref[...], nw_ref[...]) * (1.0 + mod_ref[1:2, :]) + mod_ref[0:1, :]
    hb = h.astype(BF16)
    h_ref[...] = hb
    ab_ref[...] = _dot(hb, wab_ref[...])


def _prenorm(x2, mod3, norm_w, w_ab, seq, tm):
    m, d = x2.shape
    nab = w_ab.shape[1]
    return pl.pallas_call(
        _prenorm_kernel,
        grid=(m // tm,),
        in_specs=[pl.BlockSpec((tm, d), lambda i: (i, 0)),
                  pl.BlockSpec((None, 6, d), lambda i: ((i * tm) // seq, 0, 0)),
                  pl.BlockSpec((1, d), lambda i: (0, 0)),
                  pl.BlockSpec((d, nab), lambda i: (0, 0))],
        out_specs=[pl.BlockSpec((tm, d), lambda i: (i, 0)),
                   pl.BlockSpec((tm, nab), lambda i: (i, 0))],
        out_shape=[jax.ShapeDtypeStruct((m, d), BF16),
                   jax.ShapeDtypeStruct((m, nab), F32)],
        compiler_params=_params("arbitrary"),
        name="prenorm",
    )(x2, mod3, norm_w, w_ab)


def _proj_kernel(*refs, act, conv, l2_scale, seq, group):
    if conv:
        h_ref, w_ref, cw_ref, cb_ref, o_ref, tail_scr = refs
    else:
        h_ref, w_ref, o_ref = refs
    tm = h_ref.shape[0]
    tn = w_ref.shape[1]
    panel = min(tn, V7X_MXU_COLS)
    j = pl.program_id(1)
    if conv:
        new_seq = (pl.program_id(0) * tm) % seq == 0
    for p0 in range(0, tn, panel):
        cols = slice(p0, p0 + panel)
        y = _dot(h_ref[...], w_ref[:, cols])
        if conv:
            prev = jnp.where(new_seq, 0.0, tail_scr[j, :, cols])
            tail_scr[j, :, cols] = y[tm - V7X_SUBLANES:, :]
            y = _causal_conv(prev, y, cw_ref[:, cols]) + cb_ref[:, cols]
        if act == "silu":
            y = _silu(y)
        elif act == "gelu":
            y = _gelu_tanh(y)
        if l2_scale is None:
            o_ref[:, cols] = y.astype(o_ref.dtype)
        else:
            for g0 in range(0, panel, group):
                yg = y[:, g0:g0 + group]
                yg = yg * (lax.rsqrt(jnp.sum(yg * yg, axis=-1, keepdims=True) + EPS) * l2_scale)
                o_ref[:, p0 + g0:p0 + g0 + group] = yg.astype(o_ref.dtype)


def _proj(h, w_main, col_base, n_cols, seq, tm, tn, act=None, conv_w=None, conv_b=None, l2_scale=None,
          group=V7X_LANES, name="proj"):
    m, d = h.shape
    assert col_base % tn == 0 and n_cols % tn == 0 and seq % tm == 0
    jb = col_base // tn
    conv = conv_w is not None
    kern = functools.partial(_proj_kernel, act=act, conv=conv, l2_scale=l2_scale, seq=seq, group=group)
    in_specs = [pl.BlockSpec((tm, d), lambda i, j: (i, 0)),
                pl.BlockSpec((d, tn), lambda i, j: (0, jb + j))]
    args = [h, w_main]
    scratch = []
    if conv:
        if conv_b is None:
            conv_b = jnp.zeros((1, n_cols), F32)
        in_specs += [pl.BlockSpec((CONV_W, tn), lambda i, j: (0, j)),
                     pl.BlockSpec((1, tn), lambda i, j: (0, j))]
        args += [conv_w, conv_b]
        scratch = [pltpu.VMEM((n_cols // tn, V7X_SUBLANES, tn), F32)]
    return pl.pallas_call(
        kern,
        grid=(m // tm, n_cols // tn),
        in_specs=in_specs,
        out_specs=pl.BlockSpec((tm, tn), lambda i, j: (i, j)),
        out_shape=jax.ShapeDtypeStruct((m, n_cols), BF16),
        scratch_shapes=scratch,
        compiler_params=_params("arbitrary", "arbitrary"),
        name=name,
    )(*args)


def _lru_kernel(x_ref, gate_ref, wai_ref, bai_ref, lam_ref, o_ref, a_scr, u_scr, h_scr, *, groups, tb):
    blk = V7X_LANES
    nseg = V7X_SUBLANES
    seg = tb // nseg
    pitch = seg + LRU_SEG_PAD_ROWS

    @pl.when(pl.program_id(2) == 0)
    def _():
        h_scr[...] = jnp.zeros_like(h_scr)

    sp = _softplus(-lam_ref[...])
    for g in range(groups):
        sl = slice(g * blk, (g + 1) * blk)
        xg = x_ref[:, sl].astype(F32)
        ri = _dot(x_ref[:, sl], wai_ref[g]) + bai_ref[g]
        r = _sigmoid(ri[:, :blk])
        i = _sigmoid(ri[:, blk:])
        log_a = (-LRU_C) * r * sp[:, sl]
        a = jnp.exp(log_a)
        u = jnp.sqrt(-jnp.tanh(log_a) * (1.0 + a * a)) * (i * xg)
        for s in range(nseg):
            a_scr[g, s * pitch:s * pitch + seg, :] = a[s * seg:(s + 1) * seg, :]
            u_scr[g, s * pitch:s * pitch + seg, :] = u[s * seg:(s + 1) * seg, :]

    def step(i, carry):
        out = []
        for g in range(groups):
            p, h = carry[g]
            idx = pl.ds(i, nseg, stride=pitch)
            a = a_scr[g, idx, :]
            u = u_scr[g, idx, :]
            p = a * p
            h = a * h + u
            a_scr[g, idx, :] = p
            u_scr[g, idx, :] = h
            out.append((p, h))
        return tuple(out)

    init = tuple((jnp.ones((nseg, blk), F32), jnp.zeros((nseg, blk), F32)) for _ in range(groups))
    ends = lax.fori_loop(0, seg, step, init, unroll=8)

    for g in range(groups):
        sl = slice(g * blk, (g + 1) * blk)
        p_end, h_end = ends[g]
        carry = h_scr[0:1, sl]
        for s in range(nseg):
            rows = slice(s * pitch, s * pitch + seg)
            h = u_scr[g, rows, :] + a_scr[g, rows, :] * carry
            o_ref[s * seg:(s + 1) * seg, sl] = (h * gate_ref[s * seg:(s + 1) * seg, sl].astype(F32)).astype(o_ref.dtype)
            carry = p_end[s:s + 1, :] * carry + h_end[s:s + 1, :]
        h_scr[0:1, sl] = carry


def _lru(xa3, gate3, w_ai, b_ai, lam, groups, tb):
    b, t, lw = xa3.shape
    blk = V7X_LANES
    w = groups * blk
    pitch = tb // V7X_SUBLANES + LRU_SEG_PAD_ROWS
    kern = functools.partial(_lru_kernel, groups=groups, tb=tb)
    blk_spec = pl.BlockSpec((None, tb, w), lambda bi, ci, ti: (bi, ti, ci))
    seg_scr = pltpu.VMEM((groups, V7X_SUBLANES * pitch, blk), F32)
    return pl.pallas_call(
        kern,
        grid=(b, lw // w, t // tb),
        in_specs=[blk_spec, blk_spec,
                  pl.BlockSpec((groups, blk, 2 * blk), lambda bi, ci, ti: (ci, 0, 0)),
                  pl.BlockSpec((groups, 1, 2 * blk), lambda bi, ci, ti: (ci, 0, 0)),
                  pl.BlockSpec((1, w), lambda bi, ci, ti: (0, ci))],
        out_specs=blk_spec,
        out_shape=jax.ShapeDtypeStruct((b, t, lw), BF16),
        scratch_shapes=[seg_scr, seg_scr, pltpu.VMEM((V7X_SUBLANES, w), F32)],
        compiler_params=_params("arbitrary", "arbitrary", "arbitrary"),
        name="rglru",
    )(xa3, gate3, w_ai, b_ai, lam)


def _gdn_prep_kernel(ab_ref, alog_ref, dtb_ref, gcol_ref, grow_ref, *, heads, chunk):
    t = ab_ref.shape[0]
    ab = ab_ref[...]
    g = -jnp.exp(alog_ref[...]) * _softplus(ab + dtb_ref[...])
    lane = lax.broadcasted_iota(jnp.int32, ab.shape, 1)
    row = lax.broadcasted_iota(jnp.int32, ab.shape, 0)
    pos = row & (chunk - 1)
    s = 1
    while s < chunk:
        g = g + jnp.where(pos >= s, pltpu.roll(g, s, 0), 0.0)
        s *= 2
    gcol_ref[...] = jnp.where(lane < heads, g, _sigmoid(ab))
    for c in range(t // chunk):
        grow_ref[c] = jnp.transpose(g[c * chunk:(c + 1) * chunk, :])[0:2 * V7X_SUBLANES, :]


def _gdn_prep(ab3, alog_pad, dtb_pad, heads, chunk):
    b, t, lanes = ab3.shape
    assert chunk == lanes and heads <= 2 * V7X_SUBLANES
    kern = functools.partial(_gdn_prep_kernel, heads=heads, chunk=chunk)
    return pl.pallas_call(
        kern,
        grid=(b,),
        in_specs=[pl.BlockSpec((None, t, lanes), lambda bi: (bi, 0, 0)),
                  pl.BlockSpec((1, lanes), lambda bi: (0, 0)),
                  pl.BlockSpec((1, lanes), lambda bi: (0, 0))],
        out_specs=[pl.BlockSpec((None, t, lanes), lambda bi: (bi, 0, 0)),
                   pl.BlockSpec((None, t // chunk, 2 * V7X_SUBLANES, chunk), lambda bi: (bi, 0, 0, 0))],
        out_shape=[jax.ShapeDtypeStruct((b, t, lanes), F32),
                   jax.ShapeDtypeStruct((b, t // chunk, 2 * V7X_SUBLANES, chunk), F32)],
        compiler_params=_params("arbitrary"),
        name="gdn_prep",
    )(ab3, alog_pad, dtb_pad)


def _gdn_gates_kernel(q_ref, k_ref, v_ref, z_ref, gcol_ref, grow_ref, nw_ref, h_ref, wg_ref,
                      o_ref, gates_ref, s_scr, *, heads, nh, chunk, dk, n_hg, n_t):
    c_len = chunk
    tb = q_ref.shape[0]
    n_chunks = tb // c_len
    step = pl.program_id(0)
    head0 = ((step // n_t) % n_hg) * nh

    @pl.when(step % n_t == 0)
    def _():
        s_scr[...] = jnp.zeros_like(s_scr)

    panel = min(wg_ref.shape[1], V7X_MXU_COLS)
    panels = list(range(0, wg_ref.shape[1], panel))
    levels = c_len.bit_length() - 2
    n_slots = 4 + 2 * levels + n_chunks
    due = {(2 * i + 1) * n_slots // (2 * len(panels)): p0 for i, p0 in enumerate(panels)}
    assert len(due) == len(panels)
    slot = [0]

    def between():
        p0 = due.get(slot[0])
        slot[0] += 1
        if p0 is not None:
            cols = slice(p0, p0 + panel)
            gates_ref[:, cols] = _sigmoid(_dot(h_ref[...], wg_ref[:, cols])).astype(gates_ref.dtype)

    lane = lax.broadcasted_iota(jnp.int32, (c_len, V7X_LANES), 1)
    ri = lax.broadcasted_iota(jnp.int32, (c_len, c_len), 0)
    ci = lax.broadcasted_iota(jnp.int32, (c_len, c_len), 1)

    pairs = [(c, h) for c in range(n_chunks) for h in range(nh)]
    loc = {}
    for c, h in pairs:
        rows = slice(c * c_len, (c + 1) * c_len)
        cols = slice(h * dk, (h + 1) * dk)
        head = head0 + h
        gb = gcol_ref[rows, :]
        k_bf = k_ref[rows, cols]
        q = q_ref[rows, cols].astype(F32)
        k = k_bf.astype(F32)
        g_col = jnp.sum(jnp.where(lane == head, gb, 0.0), axis=-1, keepdims=True)
        b_col = jnp.sum(jnp.where(lane == head + heads, gb, 0.0), axis=-1, keepdims=True)
        g_row = grow_ref[c, pl.ds(head, 1), :]
        e_g = jnp.exp(g_col)
        g_last = g_col[c_len - 1:c_len, :]
        kb = k * b_col
        loc[c, h] = dict(
            decay=jnp.exp(jnp.where(ri >= ci, g_col - g_row, -jnp.inf)),
            kq_lhs=jnp.concatenate([kb.astype(BF16), q_ref[rows, cols]], axis=0), k_bf=k_bf,
            rhs=jnp.concatenate([v_ref[rows, cols].astype(F32) * b_col, kb * e_g], axis=1),
            qe=q * e_g, kd=k * jnp.exp(g_last - g_col), e_last=jnp.exp(g_last))
    between()

    for p in pairs:
        loc[p]["kq"] = _dot_nt(loc[p]["kq_lhs"], loc[p]["k_bf"])
    between()
    for p in pairs:
        d = loc[p]
        d["x"] = jnp.where(ri > ci, -(d["kq"][:c_len] * d["decay"]), 0.0)
        d["attn"] = d["kq"][c_len:] * d["decay"]
        d["pow"] = d["x"].astype(BF16)
        d["t_off"] = d["x"]
    for _ in range(levels):
        for p in pairs:
            loc[p]["pow"] = _dot(loc[p]["pow"], loc[p]["pow"]).astype(BF16)
        between()
        for p in pairs:
            d = loc[p]
            p_bf = jnp.where(ri == ci, 1.0, d["t_off"]).astype(BF16)
            d["t_off"] = d["t_off"] + _dot(p_bf, d["pow"])
        between()
    for p in pairs:
        d = loc[p]
        d["t_bf"] = d["t_off"].astype(BF16)
        d["sol"] = d["rhs"] + _dot(d["t_bf"], d["rhs"].astype(BF16))
    between()
    for p in pairs:
        d = loc[p]
        d["res"] = (d["rhs"] - d["sol"]) + _dot_split(d["x"], d["sol"])
    between()
    for p in pairs:
        d = loc[p]
        sol = d["sol"] + (d["res"] + _dot(d["t_bf"], d["res"].astype(BF16)))
        d["u"] = sol[:, :dk]
        d["lhs_s"] = jnp.concatenate([sol[:, dk:], d["qe"]], axis=0).astype(BF16)
        d["lhs_v"] = jnp.concatenate([d["attn"], jnp.transpose(d["kd"])], axis=0).astype(BF16)

    state = [s_scr[h] for h in range(nh)]
    for c in range(n_chunks):
        rows = slice(c * c_len, (c + 1) * c_len)
        r_s = [_dot(loc[c, h]["lhs_s"], state[h].astype(BF16)) for h in range(nh)]
        v_new = [loc[c, h]["u"] - r_s[h][:c_len] for h in range(nh)]
        r_v = [_dot(loc[c, h]["lhs_v"], v_new[h].astype(BF16)) for h in range(nh)]
        for h in range(nh):
            cols = slice(h * dk, (h + 1) * dk)
            state[h] = state[h] * loc[c, h]["e_last"] + r_v[h][c_len:]
            o = r_s[h][c_len:] + r_v[h][:c_len]
            o = _rms(o, nw_ref[...]) * z_ref[rows, cols].astype(F32)
            o_ref[rows, cols] = o.astype(o_ref.dtype)
        between()
    assert slot[0] == n_slots
    for h in range(nh):
        s_scr[h] = state[h]


def _gdn_gates(q3, k3, v3, z3, gcol, grow, norm_w, h2d, w_main, gate_col0, n_gate_cols,
               heads, dk, dv, chunk, nh, tb, tm):
    b, t, _ = q3.shape
    m, d = h2d.shape
    assert dk == V7X_LANES and dv == V7X_LANES and heads % nh == 0 and t % tb == 0 and tb % chunk == 0
    w = nh * dk
    n_hg, n_t = heads // nh, t // tb
    steps = b * n_hg * n_t
    assert (m // tm) * n_gate_cols % steps == 0
    tn = (m // tm) * n_gate_cols // steps
    assert tn % min(tn, V7X_MXU_COLS) == 0 and n_gate_cols % tn == 0 and gate_col0 % tn == 0
    n_j = n_gate_cols // tn
    jb = gate_col0 // tn
    kern = functools.partial(_gdn_gates_kernel, heads=heads, nh=nh, chunk=chunk, dk=dk, n_hg=n_hg, n_t=n_t)

    def gdn_idx(s):
        return s // (n_hg * n_t), s % n_t, (s // n_t) % n_hg

    col = pl.BlockSpec((None, tb, w), lambda s: gdn_idx(s))
    return pl.pallas_call(
        kern,
        grid=(steps,),
        in_specs=[col, col, col, col,
                  pl.BlockSpec((None, tb, V7X_LANES), lambda s: (gdn_idx(s)[0], gdn_idx(s)[1], 0)),
                  pl.BlockSpec((None, tb // chunk, 2 * V7X_SUBLANES, chunk),
                               lambda s: (gdn_idx(s)[0], gdn_idx(s)[1], 0, 0)),
                  pl.BlockSpec((1, dv), lambda s: (0, 0)),
                  pl.BlockSpec((tm, d), lambda s: (s // n_j, 0)),
                  pl.BlockSpec((d, tn), lambda s: (0, jb + s % n_j))],
        out_specs=[col, pl.BlockSpec((tm, tn), lambda s: (s // n_j, s % n_j))],
        out_shape=[jax.ShapeDtypeStruct((b, t, heads * dv), BF16),
                   jax.ShapeDtypeStruct((m, n_gate_cols), BF16)],
        scratch_shapes=[pltpu.VMEM((nh, dk, dv), F32)],
        compiler_params=_params("arbitrary"),
        name="gdn_gates",
    )(q3, k3, v3, z3, gcol, grow, norm_w, h2d, w_main)


def _merge_kernel(ya_ref, yb_ref, wl_ref, wg_ref, gl_ref, gg_ref, o_ref):
    pa = _dot(ya_ref[...], wl_ref[...])
    pb = _dot(yb_ref[...], wg_ref[...])
    o_ref[...] = (gl_ref[...].astype(F32) * pa + gg_ref[...].astype(F32) * pb).astype(o_ref.dtype)


def _merge(ya2, yb2, w_l, w_g, gates, tm, tn):
    m, ka = ya2.shape
    kb = yb2.shape[1]
    d = w_l.shape[1]
    assert d % tn == 0
    gg0 = d // tn
    return pl.pallas_call(
        _merge_kernel,
        grid=(m // tm, d // tn),
        in_specs=[pl.BlockSpec((tm, ka), lambda i, j: (i, 0)),
                  pl.BlockSpec((tm, kb), lambda i, j: (i, 0)),
                  pl.BlockSpec((ka, tn), lambda i, j: (0, j)),
                  pl.BlockSpec((kb, tn), lambda i, j: (0, j)),
                  pl.BlockSpec((tm, tn), lambda i, j: (i, j)),
                  pl.BlockSpec((tm, tn), lambda i, j: (i, gg0 + j))],
        out_specs=pl.BlockSpec((tm, tn), lambda i, j: (i, j)),
        out_shape=jax.ShapeDtypeStruct((m, d), BF16),
        compiler_params=_params("arbitrary", "arbitrary"),
        name="merge",
    )(ya2, yb2, w_l, w_g, gates, gates)


def _mm_kernel(a_ref, w_ref, o_ref, *, relu2):
    y = _dot(a_ref[...], w_ref[...])
    if relu2:
        y = jnp.square(jnp.maximum(y, 0.0))
    o_ref[...] = y.astype(o_ref.dtype)


def _mm(a, w, out_dtype, tm, tn, relu2=False, name="mm"):
    m, k = a.shape
    n = w.shape[1]
    return pl.pallas_call(
        functools.partial(_mm_kernel, relu2=relu2),
        grid=(m // tm, n // tn),
        in_specs=[pl.BlockSpec((tm, k), lambda i, j: (i, 0)),
                  pl.BlockSpec((k, tn), lambda i, j: (0, j))],
        out_specs=pl.BlockSpec((tm, tn), lambda i, j: (i, j)),
        out_shape=jax.ShapeDtypeStruct((m, n), out_dtype),
        compiler_params=_params("arbitrary", "arbitrary"),
        name=name,
    )(a, w)


def _mm_acc_kernel(a_ref, w_ref, o_ref, acc_ref):
    kk = pl.program_id(2)

    @pl.when(kk == 0)
    def _():
        acc_ref[...] = jnp.zeros_like(acc_ref)

    acc_ref[...] += _dot(a_ref[...], w_ref[...])

    @pl.when(kk == pl.num_programs(2) - 1)
    def _():
        o_ref[...] = acc_ref[...].astype(o_ref.dtype)


def _mm_acc(a, w, out_dtype, tm, tn, tk, name="mm_acc"):
    m, k = a.shape
    n = w.shape[1]
    return pl.pallas_call(
        _mm_acc_kernel,
        grid=(m // tm, n // tn, k // tk),
        in_specs=[pl.BlockSpec((tm, tk), lambda i, j, kk: (i, kk)),
                  pl.BlockSpec((tk, tn), lambda i, j, kk: (kk, j))],
        out_specs=pl.BlockSpec((tm, tn), lambda i, j, kk: (i, j)),
        out_shape=jax.ShapeDtypeStruct((m, n), out_dtype),
        scratch_shapes=[pltpu.VMEM((tm, tn), F32)],
        compiler_params=_params("arbitrary", "arbitrary", "arbitrary"),
        name=name,
    )(a, w)


def _resid_prenorm_kernel(y_ref, x_ref, mod_ref, pw_ref, nw_ref, x1_ref, h_ref):
    x1 = x_ref[...] + mod_ref[2:3, :] * _rms(y_ref[...].astype(F32), pw_ref[...])
    x1_ref[...] = x1
    h = _rms(x1, nw_ref[...]) * (1.0 + mod_ref[4:5, :]) + mod_ref[3:4, :]
    h_ref[...] = h.astype(h_ref.dtype)


def _resid_prenorm(y, x2, mod3, post_w, pre_w, seq, tm):
    m, d = x2.shape
    row = pl.BlockSpec((tm, d), lambda i: (i, 0))
    vec = pl.BlockSpec((1, d), lambda i: (0, 0))
    return pl.pallas_call(
        _resid_prenorm_kernel,
        grid=(m // tm,),
        in_specs=[row, row, pl.BlockSpec((None, 6, d), lambda i: ((i * tm) // seq, 0, 0)), vec, vec],
        out_specs=[row, row],
        out_shape=[jax.ShapeDtypeStruct((m, d), F32), jax.ShapeDtypeStruct((m, d), BF16)],
        compiler_params=_params("arbitrary"),
        name="resid_prenorm",
    )(y, x2, mod3, post_w, pre_w)


def _resid_kernel(y_ref, x_ref, mod_ref, pw_ref, o_ref):
    o_ref[...] = x_ref[...] + mod_ref[5:6, :] * _rms(y_ref[...].astype(F32), pw_ref[...])


def _resid(y, x1, mod3, post_w, seq, tm):
    m, d = x1.shape
    row = pl.BlockSpec((tm, d), lambda i: (i, 0))
    return pl.pallas_call(
        _resid_kernel,
        grid=(m // tm,),
        in_specs=[row, row, pl.BlockSpec((None, 6, d), lambda i: ((i * tm) // seq, 0, 0)),
                  pl.BlockSpec((1, d), lambda i: (0, 0))],
        out_specs=row,
        out_shape=jax.ShapeDtypeStruct((m, d), F32),
        compiler_params=_params("arbitrary"),
        name="resid",
    )(y, x1, mod3, post_w)


def _tile(n, pref):
    t = min(n, pref)
    while n % t:
        t //= 2
    return t


def _layer(x2, mod3, batch, seq, mix_pre_norm, mix_post_norm, w_in, lru_conv_w, lru_conv_b,
           lru_gate_a_w, lru_gate_a_b, lru_gate_i_w, lru_gate_i_b, lru_lambda, gdn_conv_w,
           gdn_a_log, gdn_dt_bias, gdn_out_norm, w_branch_lru, w_branch_gdn, w_out,
           mlp_pre_norm, mlp_post_norm, w_mlp_up, w_mlp_down):
    m, d = x2.shape
    lw = lru_lambda.shape[-1]
    nblk = lru_gate_a_w.shape[0]
    heads = gdn_a_log.shape[-1]
    dv = gdn_out_norm.shape[-1]
    val = heads * dv
    conv_dim = gdn_conv_w.shape[-1]
    key = (conv_dim - val) // 2
    dk = key // heads
    lanes = V7X_LANES

    o_q = 2 * lw
    o_z = o_q + conv_dim
    o_a = o_z + val
    n_main = o_a + 2 * d
    w_in_t = jnp.swapaxes(w_in, 0, 1)
    w_main = _wprep(w_in_t, n_main, o_a, 2 * heads, _tile(d, 1024), _tile(o_a, 1024))
    w_ab = jnp.pad(w_in[:, o_a:o_a + 2 * heads], ((0, 0), (0, lanes - 2 * heads))).astype(BF16)

    h, ab = _prenorm(x2, mod3, mix_pre_norm[None, :], w_ab, seq, _tile(m, 256))

    tm = _tile(seq, 1024)
    tn = _tile(min(lw, key, val), 1024)
    proj = functools.partial(_proj, h, w_main, seq=seq, tm=tm, tn=tn)
    xa = proj(0, lw, conv_w=lru_conv_w, conv_b=lru_conv_b[None, :], name="proj_lru_x")
    gel = proj(lw, lw, act="gelu", name="proj_lru_gate")
    gq = proj(o_q, key, act="silu", conv_w=gdn_conv_w[:, :key], l2_scale=dk ** -0.5, group=dk, name="proj_q")
    gk = proj(o_q + key, key, act="silu", conv_w=gdn_conv_w[:, key:2 * key], l2_scale=1.0, group=dk, name="proj_k")
    gv = proj(o_q + 2 * key, val, act="silu", conv_w=gdn_conv_w[:, 2 * key:], name="proj_v")
    gz = proj(o_z, val, act="silu", name="proj_z")

    groups = 4 if nblk % 4 == 0 else 1
    w_ai = jnp.concatenate([lru_gate_a_w, lru_gate_i_w], axis=-1).astype(BF16)
    b_ai = jnp.concatenate([lru_gate_a_b, lru_gate_i_b], axis=-1)[:, None, :]
    as3 = lambda a: a.reshape(batch, seq, a.shape[-1])
    ya = _lru(as3(xa), as3(gel), w_ai, b_ai, lru_lambda[None, :], groups, _tile(seq, LRU_ROWS_PER_STEP))

    pad_h = lambda v: jnp.pad(v, (0, lanes - heads))[None, :]
    gcol, grow = _gdn_prep(as3(ab), pad_h(gdn_a_log), pad_h(gdn_dt_bias), heads, GDN_CHUNK)
    nh = GDN_HEADS_PER_STEP if heads % GDN_HEADS_PER_STEP == 0 else 1
    yb, gates = _gdn_gates(as3(gq), as3(gk), as3(gv), as3(gz), gcol, grow, gdn_out_norm[None, :],
                           h, w_main, o_a, 2 * d, heads, dk, dv, GDN_CHUNK, nh,
                           _tile(seq, GDN_ROWS_PER_STEP), tm)

    tm_mid = _tile(m, 512)
    merged = _merge(ya.reshape(m, lw), yb.reshape(m, val), w_branch_lru.astype(BF16),
                    w_branch_gdn.astype(BF16), gates, tm_mid, _tile(d, 1024))
    tm_big = _tile(m, 1024)
    y1 = _mm(merged, w_out.astype(BF16), BF16, tm_big, _tile(d, 1024), name="out_proj")
    x1, h2 = _resid_prenorm(y1, x2, mod3, mix_post_norm[None, :], mlp_pre_norm[None, :], seq, _tile(m, 256))

    d_ff = w_mlp_up.shape[1]
    hid = _mm(h2, w_mlp_up.astype(BF16), BF16, tm_big, _tile(d_ff, 1024), relu2=True, name="mlp_up")
    y2 = _mm_acc(hid, w_mlp_down.astype(BF16), BF16, tm_big, _tile(d, 2048), _tile(d_ff, 2048), name="mlp_down")
    return _resid(y2, x1, mod3, mlp_post_norm[None, :], seq, _tile(m, 256))


def kernel(x, c, w_ada, b_ada, mix_pre_norm, mix_post_norm, w_in, lru_conv_w, lru_conv_b, lru_gate_a_w, lru_gate_a_b, lru_gate_i_w, lru_gate_i_b, lru_lambda, gdn_conv_w, gdn_a_log, gdn_dt_bias, gdn_out_norm, w_branch_lru, w_branch_gdn, w_out, mlp_pre_norm, mlp_post_norm, w_mlp_up, w_mlp_down):
    batch, seq, d = x.shape
    depth = w_ada.shape[0]
    x2 = x.reshape(batch * seq, d)
    c_pad = jnp.pad(c, ((0, V7X_SUBLANES - batch % V7X_SUBLANES), (0, 0))) if batch % V7X_SUBLANES else c
    for l in range(depth):
        mod = _adaln(c_pad, w_ada[l], b_ada[l][None, :], _tile(6 * d, 512))
        mod3 = mod[:batch].reshape(batch, 6, d)
        x2 = _layer(x2, mod3, batch, seq, mix_pre_norm[l], mix_post_norm[l], w_in[l], lru_conv_w[l],
                    lru_conv_b[l], lru_gate_a_w[l], lru_gate_a_b[l], lru_gate_i_w[l], lru_gate_i_b[l],
                    lru_lambda[l], gdn_conv_w[l], gdn_a_log[l], gdn_dt_bias[l], gdn_out_norm[l],
                    w_branch_lru[l], w_branch_gdn[l], w_out[l], mlp_pre_norm[l], mlp_post_norm[l],
                    w_mlp_up[l], w_mlp_down[l])
    return x2.reshape(batch, seq, d)
```

```python
import functools

import jax
import jax.numpy as jnp
from jax import lax
from jax.experimental import pallas as pl
from jax.experimental.pallas import tpu as pltpu

F32 = jnp.float32
BF16 = jnp.bfloat16

EPS = 1e-6
CONV_W = 4
LRU_C = 8.0
V7X_LANES = 128
V7X_SUBLANES = 8
V7X_MXU_COLS = 256
V7X_VMEM_LIMIT_BYTES = 56 * 1024 * 1024
GDN_CHUNK = 128
GDN_HEADS_PER_STEP = 4
GDN_ROWS_PER_STEP = 512
LRU_ROWS_PER_STEP = 512
LRU_SEG_PAD_ROWS = 8


def _params(*sem):
    return pltpu.CompilerParams(dimension_semantics=sem, vmem_limit_bytes=V7X_VMEM_LIMIT_BYTES)


def _sigmoid(x):
    return 1.0 / (1.0 + jnp.exp(-x))


def _silu(x):
    return x * _sigmoid(x)


def _softplus(x):
    return jnp.maximum(x, 0.0) + jnp.log1p(jnp.exp(-jnp.abs(x)))


def _gelu_tanh(x):
    c = 0.7978845608028654
    return 0.5 * x * (1.0 + jnp.tanh(c * (x + 0.044715 * (x * x * x))))


def _rms(x, w):
    return x * lax.rsqrt(jnp.mean(x * x, axis=-1, keepdims=True) + EPS) * w


def _causal_conv(prev8, cur, w):
    rows = cur.shape[0]
    xw = jnp.concatenate([prev8, cur], axis=0)
    out = w[3:4, :] * cur
    for back in (1, 2, 3):
        out = out + w[3 - back:4 - back, :] * pltpu.roll(xw, back, 0)[V7X_SUBLANES:V7X_SUBLANES + rows, :]
    return out


def _dot(a, b):
    return jnp.dot(a, b, preferred_element_type=F32)


def _dot_split(a, b):
    a_hi = a.astype(BF16)
    b_hi = b.astype(BF16)
    a_lo = (a - a_hi.astype(F32)).astype(BF16)
    b_lo = (b - b_hi.astype(F32)).astype(BF16)
    stacked = _dot(jnp.concatenate([a_hi, a_lo], axis=1), jnp.concatenate([b_hi, b_hi], axis=0))
    return stacked + _dot(a_hi, b_lo)


def _dot_nt(a, b):
    return lax.dot_general(a, b, (((1,), (1,)), ((), ())), preferred_element_type=F32)


def _adaln_kernel(c_ref, w_ref, b_ref, o_ref):
    ca = _silu(c_ref[...])
    o_ref[...] = _dot(ca.astype(BF16), w_ref[...].astype(BF16)) + b_ref[...]


def _adaln(c_pad, w_ada, b_ada, tn):
    rows, d = c_pad.shape
    n = w_ada.shape[1]
    return pl.pallas_call(
        _adaln_kernel,
        grid=(n // tn,),
        in_specs=[pl.BlockSpec((rows, d), lambda j: (0, 0)),
                  pl.BlockSpec((d, tn), lambda j: (0, j)),
                  pl.BlockSpec((1, tn), lambda j: (0, j))],
        out_specs=pl.BlockSpec((rows, tn), lambda j: (0, j)),
        out_shape=jax.ShapeDtypeStruct((rows, n), F32),
        compiler_params=_params("arbitrary"),
        name="adaln",
    )(c_pad, w_ada, b_ada)


def _wprep_kernel(a_ref, b_ref, o_ref, *, n_aligned, skip):
    j = pl.program_id(1)

    @pl.when(j < n_aligned)
    def _():
        o_ref[...] = jnp.transpose(a_ref[...]).astype(o_ref.dtype)

    @pl.when(j >= n_aligned)
    def _():
        src = jnp.concatenate([a_ref[skip:, :], b_ref[...]], axis=0)
        o_ref[...] = jnp.transpose(src).astype(o_ref.dtype)


def _wprep(w_in_t, n_out, n_aligned_rows, skip, tk, tn):
    n_src, k = w_in_t.shape
    assert n_aligned_rows % tn == 0 and n_out % tn == 0 and tn % skip == 0 and skip % V7X_SUBLANES == 0
    assert n_src == n_out + skip
    kern = functools.partial(_wprep_kernel, n_aligned=n_aligned_rows // tn, skip=skip)
    return pl.pallas_call(
        kern,
        grid=(k // tk, n_out // tn),
        in_specs=[pl.BlockSpec((tn, tk), lambda i, j: (j, i)),
                  pl.BlockSpec((skip, tk), lambda i, j: ((j + 1) * (tn // skip), i))],
        out_specs=pl.BlockSpec((tk, tn), lambda i, j: (i, j)),
        out_shape=jax.ShapeDtypeStruct((k, n_out), BF16),
        compiler_params=_params("arbitrary", "arbitrary"),
        name="wprep",
    )(w_in_t, w_in_t)


def _prenorm_kernel(x_ref, mod_ref, nw_ref, wab_ref, h_ref, ab_ref):
    h = _rms(x_ref[...], nw_ref[...]) * (1.0 + mod_ref[1:2, :]) + mod_ref[0:1, :]
    hb = h.astype(BF16)
    h_ref[...] = hb
    ab_ref[...] = _dot(hb, wab_ref[...])


def _prenorm(x2, mod3, norm_w, w_ab, seq, tm):
    m, d = x2.shape
    nab = w_ab.shape[1]
    return pl.pallas_call(
        _prenorm_kernel,
        grid=(m // tm,),
        in_specs=[pl.BlockSpec((tm, d), lambda i: (i, 0)),
                  pl.BlockSpec((None, 6, d), lambda i: ((i * tm) // seq, 0, 0)),
                  pl.BlockSpec((1, d), lambda i: (0, 0)),
                  pl.BlockSpec((d, nab), lambda i: (0, 0))],
        out_specs=[pl.BlockSpec((tm, d), lambda i: (i, 0)),
                   pl.BlockSpec((tm, nab), lambda i: (i, 0))],
        out_shape=[jax.ShapeDtypeStruct((m, d), BF16),
                   jax.ShapeDtypeStruct((m, nab), F32)],
        compiler_params=_params("arbitrary"),
        name="prenorm",
    )(x2, mod3, norm_w, w_ab)


def _proj_kernel(*refs, act, conv, l2_scale, seq, group):
    if conv:
        h_ref, w_ref, cw_ref, cb_ref, o_ref, tail_scr = refs
    else:
        h_ref, w_ref, o_ref = refs
    tm = h_ref.shape[0]
    tn = w_ref.shape[1]
    panel = min(tn, V7X_MXU_COLS)
    j = pl.program_id(1)
    if conv:
        new_seq = (pl.program_id(0) * tm) % seq == 0
    for p0 in range(0, tn, panel):
        cols = slice(p0, p0 + panel)
        y = _dot(h_ref[...], w_ref[:, cols])
        if conv:
            prev = jnp.where(new_seq, 0.0, tail_scr[j, :, cols])
            tail_scr[j, :, cols] = y[tm - V7X_SUBLANES:, :]
            y = _causal_conv(prev, y, cw_ref[:, cols]) + cb_ref[:, cols]
        if act == "silu":
            y = _silu(y)
        elif act == "gelu":
            y = _gelu_tanh(y)
        if l2_scale is None:
            o_ref[:, cols] = y.astype(o_ref.dtype)
        else:
            for g0 in range(0, panel, group):
                yg = y[:, g0:g0 + group]
                yg = yg * (lax.rsqrt(jnp.sum(yg * yg, axis=-1, keepdims=True) + EPS) * l2_scale)
                o_ref[:, p0 + g0:p0 + g0 + group] = yg.astype(o_ref.dtype)


def _proj(h, w_main, col_base, n_cols, seq, tm, tn, act=None, conv_w=None, conv_b=None, l2_scale=None,
          group=V7X_LANES, name="proj"):
    m, d = h.shape
    assert col_base % tn == 0 and n_cols % tn == 0 and seq % tm == 0
    jb = col_base // tn
    conv = conv_w is not None
    kern = functools.partial(_proj_kernel, act=act, conv=conv, l2_scale=l2_scale, seq=seq, group=group)
    in_specs = [pl.BlockSpec((tm, d), lambda i, j: (i, 0)),
                pl.BlockSpec((d, tn), lambda i, j: (0, jb + j))]
    args = [h, w_main]
    scratch = []
    if conv:
        if conv_b is None:
            conv_b = jnp.zeros((1, n_cols), F32)
        in_specs += [pl.BlockSpec((CONV_W, tn), lambda i, j: (0, j)),
                     pl.BlockSpec((1, tn), lambda i, j: (0, j))]
        args += [conv_w, conv_b]
        scratch = [pltpu.VMEM((n_cols // tn, V7X_SUBLANES, tn), F32)]
    return pl.pallas_call(
        kern,
        grid=(m // tm, n_cols // tn),
        in_specs=in_specs,
        out_specs=pl.BlockSpec((tm, tn), lambda i, j: (i, j)),
        out_shape=jax.ShapeDtypeStruct((m, n_cols), BF16),
        scratch_shapes=scratch,
        compiler_params=_params("arbitrary", "arbitrary"),
        name=name,
    )(*args)


def _lru_kernel(x_ref, gate_ref, wai_ref, bai_ref, lam_ref, o_ref, a_scr, u_scr, h_scr, *, groups, tb):
    blk = V7X_LANES
    nseg = V7X_SUBLANES
    seg = tb // nseg
    pitch = seg + LRU_SEG_PAD_ROWS

    @pl.when(pl.program_id(2) == 0)
    def _():
        h_scr[...] = jnp.zeros_like(h_scr)

    sp = _softplus(-lam_ref[...])
    for g in range(groups):
        sl = slice(g * blk, (g + 1) * blk)
        xg = x_ref[:, sl].astype(F32)
        ri = _dot(x_ref[:, sl], wai_ref[g]) + bai_ref[g]
        r = _sigmoid(ri[:, :blk])
        i = _sigmoid(ri[:, blk:])
        log_a = (-LRU_C) * r * sp[:, sl]
        a = jnp.exp(log_a)
        u = jnp.sqrt(-jnp.tanh(log_a) * (1.0 + a * a)) * (i * xg)
        for s in range(nseg):
            a_scr[g, s * pitch:s * pitch + seg, :] = a[s * seg:(s + 1) * seg, :]
            u_scr[g, s * pitch:s * pitch + seg, :] = u[s * seg:(s + 1) * seg, :]

    def step(i, carry):
        out = []
        for g in range(groups):
            p, h = carry[g]
            idx = pl.ds(i, nseg, stride=pitch)
            a = a_scr[g, idx, :]
            u = u_scr[g, idx, :]
            p = a * p
            h = a * h + u
            a_scr[g, idx, :] = p
            u_scr[g, idx, :] = h
            out.append((p, h))
        return tuple(out)

    init = tuple((jnp.ones((nseg, blk), F32), jnp.zeros((nseg, blk), F32)) for _ in range(groups))
    ends = lax.fori_loop(0, seg, step, init, unroll=8)

    for g in range(groups):
        sl = slice(g * blk, (g + 1) * blk)
        p_end, h_end = ends[g]
        carry = h_scr[0:1, sl]
        for s in range(nseg):
            rows = slice(s * pitch, s * pitch + seg)
            h = u_scr[g, rows, :] + a_scr[g, rows, :] * carry
            o_ref[s * seg:(s + 1) * seg, sl] = (h * gate_ref[s * seg:(s + 1) * seg, sl].astype(F32)).astype(o_ref.dtype)
            carry = p_end[s:s + 1, :] * carry + h_end[s:s + 1, :]
        h_scr[0:1, sl] = carry


def _lru(xa3, gate3, w_ai, b_ai, lam, groups, tb):
    b, t, lw = xa3.shape
    blk = V7X_LANES
    w = groups * blk
    pitch = tb // V7X_SUBLANES + LRU_SEG_PAD_ROWS
    kern = functools.partial(_lru_kernel, groups=groups, tb=tb)
    blk_spec = pl.BlockSpec((None, tb, w), lambda bi, ci, ti: (bi, ti, ci))
    seg_scr = pltpu.VMEM((groups, V7X_SUBLANES * pitch, blk), F32)
    return pl.pallas_call(
        kern,
        grid=(b, lw // w, t // tb),
        in_specs=[blk_spec, blk_spec,
                  pl.BlockSpec((groups, blk, 2 * blk), lambda bi, ci, ti: (ci, 0, 0)),
                  pl.BlockSpec((groups, 1, 2 * blk), lambda bi, ci, ti: (ci, 0, 0)),
                  pl.BlockSpec((1, w), lambda bi, ci, ti: (0, ci))],
        out_specs=blk_spec,
        out_shape=jax.ShapeDtypeStruct((b, t, lw), BF16),
        scratch_shapes=[seg_scr, seg_scr, pltpu.VMEM((V7X_SUBLANES, w), F32)],
        compiler_params=_params("arbitrary", "arbitrary", "arbitrary"),
        name="rglru",
    )(xa3, gate3, w_ai, b_ai, lam)


def _gdn_prep_kernel(ab_ref, alog_ref, dtb_ref, gcol_ref, grow_ref, *, heads, chunk):
    t = ab_ref.shape[0]
    ab = ab_ref[...]
    g = -jnp.exp(alog_ref[...]) * _softplus(ab + dtb_ref[...])
    lane = lax.broadcasted_iota(jnp.int32, ab.shape, 1)
    row = lax.broadcasted_iota(jnp.int32, ab.shape, 0)
    pos = row & (chunk - 1)
    s = 1
    while s < chunk:
        g = g + jnp.where(pos >= s, pltpu.roll(g, s, 0), 0.0)
        s *= 2
    gcol_ref[...] = jnp.where(lane < heads, g, _sigmoid(ab))
    for c in range(t // chunk):
        grow_ref[c] = jnp.transpose(g[c * chunk:(c + 1) * chunk, :])[0:2 * V7X_SUBLANES, :]


def _gdn_prep(ab3, alog_pad, dtb_pad, heads, chunk):
    b, t, lanes = ab3.shape
    assert chunk == lanes and heads <= 2 * V7X_SUBLANES
    kern = functools.partial(_gdn_prep_kernel, heads=heads, chunk=chunk)
    return pl.pallas_call(
        kern,
        grid=(b,),
        in_specs=[pl.BlockSpec((None, t, lanes), lambda bi: (bi, 0, 0)),
                  pl.BlockSpec((1, lanes), lambda bi: (0, 0)),
                  pl.BlockSpec((1, lanes), lambda bi: (0, 0))],
        out_specs=[pl.BlockSpec((None, t, lanes), lambda bi: (bi, 0, 0)),
                   pl.BlockSpec((None, t // chunk, 2 * V7X_SUBLANES, chunk), lambda bi: (bi, 0, 0, 0))],
        out_shape=[jax.ShapeDtypeStruct((b, t, lanes), F32),
                   jax.ShapeDtypeStruct((b, t // chunk, 2 * V7X_SUBLANES, chunk), F32)],
        compiler_params=_params("arbitrary"),
        name="gdn_prep",
    )(ab3, alog_pad, dtb_pad)


def _gdn_gates_kernel(q_ref, k_ref, v_ref, z_ref, gcol_ref, grow_ref, nw_ref, h_ref, wg_ref,
                      o_ref, gates_ref, s_scr, *, heads, nh, chunk, dk, n_hg, n_t):
    c_len = chunk
    tb = q_ref.shape[0]
    n_chunks = tb // c_len
    step = pl.program_id(0)
    head0 = ((step // n_t) % n_hg) * nh

    @pl.when(step % n_t == 0)
    def _():
        s_scr[...] = jnp.zeros_like(s_scr)

    panel = min(wg_ref.shape[1], V7X_MXU_COLS)
    panels = list(range(0, wg_ref.shape[1], panel))
    levels = c_len.bit_length() - 2
    n_slots = 4 + 2 * levels + n_chunks
    due = {(2 * i + 1) * n_slots // (2 * len(panels)): p0 for i, p0 in enumerate(panels)}
    assert len(due) == len(panels)
    slot = [0]

    def between():
        p0 = due.get(slot[0])
        slot[0] += 1
        if p0 is not None:
            cols = slice(p0, p0 + panel)
            gates_ref[:, cols] = _sigmoid(_dot(h_ref[...], wg_ref[:, cols])).astype(gates_ref.dtype)

    lane = lax.broadcasted_iota(jnp.int32, (c_len, V7X_LANES), 1)
    ri = lax.broadcasted_iota(jnp.int32, (c_len, c_len), 0)
    ci = lax.broadcasted_iota(jnp.int32, (c_len, c_len), 1)

    pairs = [(c, h) for c in range(n_chunks) for h in range(nh)]
    loc = {}
    for c, h in pairs:
        rows = slice(c * c_len, (c + 1) * c_len)
        cols = slice(h * dk, (h + 1) * dk)
        head = head0 + h
        gb = gcol_ref[rows, :]
        k_bf = k_ref[rows, cols]
        q = q_ref[rows, cols].astype(F32)
        k = k_bf.astype(F32)
        g_col = jnp.sum(jnp.where(lane == head, gb, 0.0), axis=-1, keepdims=True)
        b_col = jnp.sum(jnp.where(lane == head + heads, gb, 0.0), axis=-1, keepdims=True)
        g_row = grow_ref[c, pl.ds(head, 1), :]
        e_g = jnp.exp(g_col)
        g_last = g_col[c_len - 1:c_len, :]
        kb = k * b_col
        loc[c, h] = dict(
            decay=jnp.exp(jnp.where(ri >= ci, g_col - g_row, -jnp.inf)),
            kq_lhs=jnp.concatenate([kb.astype(BF16), q_ref[rows, cols]], axis=0), k_bf=k_bf,
            rhs=jnp.concatenate([v_ref[rows, cols].astype(F32) * b_col, kb * e_g], axis=1),
            qe=q * e_g, kd=k * jnp.exp(g_last - g_col), e_last=jnp.exp(g_last))
    between()

    for p in pairs:
        loc[p]["kq"] = _dot_nt(loc[p]["kq_lhs"], loc[p]["k_bf"])
    between()
    for p in pairs:
        d = loc[p]
        d["x"] = jnp.where(ri > ci, -(d["kq"][:c_len] * d["decay"]), 0.0)
        d["attn"] = d["kq"][c_len:] * d["decay"]
        d["pow"] = d["x"].astype(BF16)
        d["t_off"] = d["x"]
    for _ in range(levels):
        for p in pairs:
            loc[p]["pow"] = _dot(loc[p]["pow"], loc[p]["pow"]).astype(BF16)
        between()
        for p in pairs:
            d = loc[p]
            p_bf = jnp.where(ri == ci, 1.0, d["t_off"]).astype(BF16)
            d["t_off"] = d["t_off"] + _dot(p_bf, d["pow"])
        between()
    for p in pairs:
        d = loc[p]
        d["t_bf"] = d["t_off"].astype(BF16)
        d["sol"] = d["rhs"] + _dot(d["t_bf"], d["rhs"].astype(BF16))
    between()
    for p in pairs:
        d = loc[p]
        d["res"] = (d["rhs"] - d["sol"]) + _dot_split(d["x"], d["sol"])
    between()
    for p in pairs:
        d = loc[p]
        sol = d["sol"] + (d["res"] + _dot(d["t_bf"], d["res"].astype(BF16)))
        d["u"] = sol[:, :dk]
        d["lhs_s"] = jnp.concatenate([sol[:, dk:], d["qe"]], axis=0).astype(BF16)
        d["lhs_v"] = jnp.concatenate([d["attn"], jnp.transpose(d["kd"])], axis=0).astype(BF16)

    state = [s_scr[h] for h in range(nh)]
    for c in range(n_chunks):
        rows = slice(c * c_len, (c + 1) * c_len)
        r_s = [_dot(loc[c, h]["lhs_s"], state[h].astype(BF16)) for h in range(nh)]
        v_new = [loc[c, h]["u"] - r_s[h][:c_len] for h in range(nh)]
        r_v = [_dot(loc[c, h]["lhs_v"], v_new[h].astype(BF16)) for h in range(nh)]
        for h in range(nh):
            cols = slice(h * dk, (h + 1) * dk)
            state[h] = state[h] * loc[c, h]["e_last"] + r_v[h][c_len:]
            o = r_s[h][c_len:] + r_v[h][:c_len]
            o = _rms(o, nw_ref[...]) * z_ref[rows, cols].astype(F32)
            o_ref[rows, cols] = o.astype(o_ref.dtype)
        between()
    assert slot[0] == n_slots
    for h in range(nh):
        s_scr[h] = state[h]


def _gdn_gates(q3, k3, v3, z3, gcol, grow, norm_w, h2d, w_main, gate_col0, n_gate_cols,
               heads, dk, dv, chunk, nh, tb, tm):
    b, t, _ = q3.shape
    m, d = h2d.shape
    assert dk == V7X_LANES and dv == V7X_LANES and heads % nh == 0 and t % tb == 0 and tb % chunk == 0
    w = nh * dk
    n_hg, n_t = heads // nh, t // tb
    steps = b * n_hg * n_t
    assert (m // tm) * n_gate_cols % steps == 0
    tn = (m // tm) * n_gate_cols // steps
    assert tn % min(tn, V7X_MXU_COLS) == 0 and n_gate_cols % tn == 0 and gate_col0 % tn == 0
    n_j = n_gate_cols // tn
    jb = gate_col0 // tn
    kern = functools.partial(_gdn_gates_kernel, heads=heads, nh=nh, chunk=chunk, dk=dk, n_hg=n_hg, n_t=n_t)

    def gdn_idx(s):
        return s // (n_hg * n_t), s % n_t, (s // n_t) % n_hg

    col = pl.BlockSpec((None, tb, w), lambda s: gdn_idx(s))
    return pl.pallas_call(
        kern,
        grid=(steps,),
        in_specs=[col, col, col, col,
                  pl.BlockSpec((None, tb, V7X_LANES), lambda s: (gdn_idx(s)[0], gdn_idx(s)[1], 0)),
                  pl.BlockSpec((None, tb // chunk, 2 * V7X_SUBLANES, chunk),
                               lambda s: (gdn_idx(s)[0], gdn_idx(s)[1], 0, 0)),
                  pl.BlockSpec((1, dv), lambda s: (0, 0)),
                  pl.BlockSpec((tm, d), lambda s: (s // n_j, 0)),
                  pl.BlockSpec((d, tn), lambda s: (0, jb + s % n_j))],
        out_specs=[col, pl.BlockSpec((tm, tn), lambda s: (s // n_j, s % n_j))],
        out_shape=[jax.ShapeDtypeStruct((b, t, heads * dv), BF16),
                   jax.ShapeDtypeStruct((m, n_gate_cols), BF16)],
        scratch_shapes=[pltpu.VMEM((nh, dk, dv), F32)],
        compiler_params=_params("arbitrary"),
        name="gdn_gates",
    )(q3, k3, v3, z3, gcol, grow, norm_w, h2d, w_main)


def _cast_job(src, grid):
    steps = grid[0] * grid[1]
    rows, cols = src.shape
    assert rows % steps == 0 and (rows // steps) % (2 * V7X_SUBLANES) == 0
    spec = pl.BlockSpec((rows // steps, cols), lambda i, j: (i * grid[1] + j, 0))
    return spec, jax.ShapeDtypeStruct((rows, cols), BF16)


def _merge_kernel(ya_ref, yb_ref, wl_ref, wg_ref, gl_ref, gg_ref, ci_ref, o_ref, co_ref):
    pa = _dot(ya_ref[...], wl_ref[...])
    pb = _dot(yb_ref[...], wg_ref[...])
    o_ref[...] = (gl_ref[...].astype(F32) * pa + gg_ref[...].astype(F32) * pb).astype(o_ref.dtype)
    co_ref[...] = ci_ref[...].astype(co_ref.dtype)


def _merge(ya2, yb2, w_l, w_g, gates, cast_src, tm, tn):
    m, ka = ya2.shape
    kb = yb2.shape[1]
    d = w_l.shape[1]
    assert d % tn == 0
    gg0 = d // tn
    grid = (m // tm, d // tn)
    cast_spec, cast_shape = _cast_job(cast_src, grid)
    return pl.pallas_call(
        _merge_kernel,
        grid=grid,
        in_specs=[pl.BlockSpec((tm, ka), lambda i, j: (i, 0)),
                  pl.BlockSpec((tm, kb), lambda i, j: (i, 0)),
                  pl.BlockSpec((ka, tn), lambda i, j: (0, j)),
                  pl.BlockSpec((kb, tn), lambda i, j: (0, j)),
                  pl.BlockSpec((tm, tn), lambda i, j: (i, j)),
                  pl.BlockSpec((tm, tn), lambda i, j: (i, gg0 + j)),
                  cast_spec],
        out_specs=[pl.BlockSpec((tm, tn), lambda i, j: (i, j)), cast_spec],
        out_shape=[jax.ShapeDtypeStruct((m, d), BF16), cast_shape],
        compiler_params=_params("arbitrary", "arbitrary"),
        name="merge",
    )(ya2, yb2, w_l, w_g, gates, gates, cast_src)


def _mm_kernel(*refs, relu2, cast):
    if cast:
        a_ref, w_ref, ci_ref, o_ref, co_ref = refs
        co_ref[...] = ci_ref[...].astype(co_ref.dtype)
    else:
        a_ref, w_ref, o_ref = refs
    y = _dot(a_ref[...], w_ref[...])
    if relu2:
        y = jnp.square(jnp.maximum(y, 0.0))
    o_ref[...] = y.astype(o_ref.dtype)


def _mm(a, w, out_dtype, tm, tn, relu2=False, cast_src=None, name="mm"):
    m, k = a.shape
    n = w.shape[1]
    grid = (m // tm, n // tn)
    in_specs = [pl.BlockSpec((tm, k), lambda i, j: (i, 0)),
                pl.BlockSpec((k, tn), lambda i, j: (0, j))]
    out_specs = pl.BlockSpec((tm, tn), lambda i, j: (i, j))
    out_shape = jax.ShapeDtypeStruct((m, n), out_dtype)
    args = (a, w)
    if cast_src is not None:
        cast_spec, cast_shape = _cast_job(cast_src, grid)
        in_specs, args = in_specs + [cast_spec], args + (cast_src,)
        out_specs, out_shape = [out_specs, cast_spec], [out_shape, cast_shape]
    return pl.pallas_call(
        functools.partial(_mm_kernel, relu2=relu2, cast=cast_src is not None),
        grid=grid,
        in_specs=in_specs,
        out_specs=out_specs,
        out_shape=out_shape,
        compiler_params=_params("arbitrary", "arbitrary"),
        name=name,
    )(*args)


def _mm_acc_kernel(a_ref, w_ref, o_ref, acc_ref):
    kk = pl.program_id(2)

    @pl.when(kk == 0)
    def _():
        acc_ref[...] = jnp.zeros_like(acc_ref)

    acc_ref[...] += _dot(a_ref[...], w_ref[...])

    @pl.when(kk == pl.num_programs(2) - 1)
    def _():
        o_ref[...] = acc_ref[...].astype(o_ref.dtype)


def _mm_acc(a, w, out_dtype, tm, tn, tk, name="mm_acc"):
    m, k = a.shape
    n = w.shape[1]
    return pl.pallas_call(
        _mm_acc_kernel,
        grid=(m // tm, n // tn, k // tk),
        in_specs=[pl.BlockSpec((tm, tk), lambda i, j, kk: (i, kk)),
                  pl.BlockSpec((tk, tn), lambda i, j, kk: (kk, j))],
        out_specs=pl.BlockSpec((tm, tn), lambda i, j, kk: (i, j)),
        out_shape=jax.ShapeDtypeStruct((m, n), out_dtype),
        scratch_shapes=[pltpu.VMEM((tm, tn), F32)],
        compiler_params=_params("arbitrary", "arbitrary", "arbitrary"),
        name=name,
    )(a, w)


def _resid_prenorm_kernel(y_ref, x_ref, mod_ref, pw_ref, nw_ref, x1_ref, h_ref):
    x1 = x_ref[...] + mod_ref[2:3, :] * _rms(y_ref[...].astype(F32), pw_ref[...])
    x1_ref[...] = x1
    h = _rms(x1, nw_ref[...]) * (1.0 + mod_ref[4:5, :]) + mod_ref[3:4, :]
    h_ref[...] = h.astype(h_ref.dtype)


def _resid_prenorm(y, x2, mod3, post_w, pre_w, seq, tm):
    m, d = x2.shape
    row = pl.BlockSpec((tm, d), lambda i: (i, 0))
    vec = pl.BlockSpec((1, d), lambda i: (0, 0))
    return pl.pallas_call(
        _resid_prenorm_kernel,
        grid=(m // tm,),
        in_specs=[row, row, pl.BlockSpec((None, 6, d), lambda i: ((i * tm) // seq, 0, 0)), vec, vec],
        out_specs=[row, row],
        out_shape=[jax.ShapeDtypeStruct((m, d), F32), jax.ShapeDtypeStruct((m, d), BF16)],
        compiler_params=_params("arbitrary"),
        name="resid_prenorm",
    )(y, x2, mod3, post_w, pre_w)


def _resid_kernel(y_ref, x_ref, mod_ref, pw_ref, o_ref):
    o_ref[...] = x_ref[...] + mod_ref[5:6, :] * _rms(y_ref[...].astype(F32), pw_ref[...])


def _resid(y, x1, mod3, post_w, seq, tm):
    m, d = x1.shape
    row = pl.BlockSpec((tm, d), lambda i: (i, 0))
    return pl.pallas_call(
        _resid_kernel,
        grid=(m // tm,),
        in_specs=[row, row, pl.BlockSpec((None, 6, d), lambda i: ((i * tm) // seq, 0, 0)),
                  pl.BlockSpec((1, d), lambda i: (0, 0))],
        out_specs=row,
        out_shape=jax.ShapeDtypeStruct((m, d), F32),
        compiler_params=_params("arbitrary"),
        name="resid",
    )(y, x1, mod3, post_w)


def _tile(n, pref):
    t = min(n, pref)
    while n % t:
        t //= 2
    return t


def _layer(x2, mod3, batch, seq, mix_pre_norm, mix_post_norm, w_in, lru_conv_w, lru_conv_b,
           lru_gate_a_w, lru_gate_a_b, lru_gate_i_w, lru_gate_i_b, lru_lambda, gdn_conv_w,
           gdn_a_log, gdn_dt_bias, gdn_out_norm, w_branch_lru, w_branch_gdn, w_out,
           mlp_pre_norm, mlp_post_norm, w_mlp_up, w_mlp_down):
    m, d = x2.shape
    lw = lru_lambda.shape[-1]
    nblk = lru_gate_a_w.shape[0]
    heads = gdn_a_log.shape[-1]
    dv = gdn_out_norm.shape[-1]
    val = heads * dv
    conv_dim = gdn_conv_w.shape[-1]
    key = (conv_dim - val) // 2
    dk = key // heads
    lanes = V7X_LANES

    o_q = 2 * lw
    o_z = o_q + conv_dim
    o_a = o_z + val
    n_main = o_a + 2 * d
    w_in_t = jnp.swapaxes(w_in, 0, 1)
    w_main = _wprep(w_in_t, n_main, o_a, 2 * heads, _tile(d, 1024), _tile(o_a, 1024))
    w_ab = jnp.pad(w_in[:, o_a:o_a + 2 * heads], ((0, 0), (0, lanes - 2 * heads))).astype(BF16)

    h, ab = _prenorm(x2, mod3, mix_pre_norm[None, :], w_ab, seq, _tile(seq, 512))

    tm = _tile(seq, 1024)
    tn = _tile(min(lw, key, val), 1024)
    proj = functools.partial(_proj, h, w_main, seq=seq, tm=tm, tn=tn)
    xa = proj(0, lw, conv_w=lru_conv_w, conv_b=lru_conv_b[None, :], name="proj_lru_x")
    gel = proj(lw, lw, act="gelu", name="proj_lru_gate")
    gq = proj(o_q, key, act="silu", conv_w=gdn_conv_w[:, :key], l2_scale=dk ** -0.5, group=dk, name="proj_q")
    gk = proj(o_q + key, key, act="silu", conv_w=gdn_conv_w[:, key:2 * key], l2_scale=1.0, group=dk, name="proj_k")
    gv = proj(o_q + 2 * key, val, act="silu", conv_w=gdn_conv_w[:, 2 * key:], name="proj_v")
    gz = proj(o_z, val, act="silu", name="proj_z")

    groups = 4 if nblk % 4 == 0 else 1
    w_ai = jnp.concatenate([lru_gate_a_w, lru_gate_i_w], axis=-1).astype(BF16)
    b_ai = jnp.concatenate([lru_gate_a_b, lru_gate_i_b], axis=-1)[:, None, :]
    as3 = lambda a: a.reshape(batch, seq, a.shape[-1])
    ya = _lru(as3(xa), as3(gel), w_ai, b_ai, lru_lambda[None, :], groups, _tile(seq, LRU_ROWS_PER_STEP))

    pad_h = lambda v: jnp.pad(v, (0, lanes - heads))[None, :]
    gcol, grow = _gdn_prep(as3(ab), pad_h(gdn_a_log), pad_h(gdn_dt_bias), heads, GDN_CHUNK)
    nh = GDN_HEADS_PER_STEP if heads % GDN_HEADS_PER_STEP == 0 else 1
    yb, gates = _gdn_gates(as3(gq), as3(gk), as3(gv), as3(gz), gcol, grow, gdn_out_norm[None, :],
                           h, w_main, o_a, 2 * d, heads, dk, dv, GDN_CHUNK, nh,
                           _tile(seq, GDN_ROWS_PER_STEP), tm)

    tm_mid = _tile(m, 512)
    merged, w_up_bf = _merge(ya.reshape(m, lw), yb.reshape(m, val), w_branch_lru.astype(BF16),
                             w_branch_gdn.astype(BF16), gates, w_mlp_up, tm_mid, _tile(d, 1024))
    tm_big = _tile(m, 1024)
    y1, w_down_bf = _mm(merged, w_out.astype(BF16), BF16, tm_big, _tile(d, 1024), cast_src=w_mlp_down,
                        name="out_proj")
    x1, h2 = _resid_prenorm(y1, x2, mod3, mix_post_norm[None, :], mlp_pre_norm[None, :], seq, _tile(m, 256))

    d_ff = w_mlp_up.shape[1]
    hid = _mm(h2, w_up_bf, BF16, tm_big, _tile(d_ff, 1024), relu2=True, name="mlp_up")
    y2 = _mm_acc(hid, w_down_bf, BF16, tm_big, _tile(d, 2048), _tile(d_ff, 2048), name="mlp_down")
    return _resid(y2, x1, mod3, mlp_post_norm[None, :], seq, _tile(m, 256))


def kernel(x, c, w_ada, b_ada, mix_pre_norm, mix_post_norm, w_in, lru_conv_w, lru_conv_b, lru_gate_a_w, lru_gate_a_b, lru_gate_i_w, lru_gate_i_b, lru_lambda, gdn_conv_w, gdn_a_log, gdn_dt_bias, gdn_out_norm, w_branch_lru, w_branch_gdn, w_out, mlp_pre_norm, mlp_post_norm, w_mlp_up, w_mlp_down):
    batch, seq, d = x.shape
    depth = w_ada.shape[0]
    x2 = x.reshape(batch * seq, d)
    c_pad = jnp.pad(c, ((0, V7X_SUBLANES - batch % V7X_SUBLANES), (0, 0))) if batch % V7X_SUBLANES else c
    for l in range(depth):
        mod = _adaln(c_pad, w_ada[l], b_ada[l][None, :], _tile(6 * d, 512))
        mod3 = mod[:batch].reshape(batch, 6, d)
        x2 = _layer(x2, mod3, batch, seq, mix_pre_norm[l], mix_post_norm[l], w_in[l], lru_conv_w[l],
                    lru_conv_b[l], lru_gate_a_w[l], lru_gate_a_b[l], lru_gate_i_w[l], lru_gate_i_b[l],
                    lru_lambda[l], gdn_conv_w[l], gdn_a_log[l], gdn_dt_bias[l], gdn_out_norm[l],
                    w_branch_lru[l], w_branch_gdn[l], w_out[l], mlp_pre_norm[l], mlp_post_norm[l],
                    w_mlp_up[l], w_mlp_down[l])
    return x2.reshape(batch, seq, d)
```

```python
import functools

import jax
import jax.numpy as jnp
from jax import lax
from jax.experimental import pallas as pl
from jax.experimental.pallas import tpu as pltpu

F32 = jnp.float32
BF16 = jnp.bfloat16

EPS = 1e-6
CONV_W = 4
LRU_C = 8.0
V7X_LANES = 128
V7X_SUBLANES = 8
V7X_MXU_COLS = 256
V7X_VMEM_LIMIT_BYTES = 56 * 1024 * 1024
GDN_CHUNK = 128
GDN_HEADS_PER_STEP = 4
GDN_ROWS_PER_STEP = 512
LRU_ROWS_PER_STEP = 512
LRU_SEG_PAD_ROWS = 8


def _params(*sem):
    return pltpu.CompilerParams(dimension_semantics=sem, vmem_limit_bytes=V7X_VMEM_LIMIT_BYTES)


def _sigmoid(x):
    return 1.0 / (1.0 + jnp.exp(-x))


def _silu(x):
    return x * _sigmoid(x)


def _softplus(x):
    return jnp.maximum(x, 0.0) + jnp.log1p(jnp.exp(-jnp.abs(x)))


def _gelu_tanh(x):
    c = 0.7978845608028654
    return 0.5 * x * (1.0 + jnp.tanh(c * (x + 0.044715 * (x * x * x))))


def _rms(x, w):
    return x * lax.rsqrt(jnp.mean(x * x, axis=-1, keepdims=True) + EPS) * w


def _causal_conv(prev8, cur, w):
    rows = cur.shape[0]
    xw = jnp.concatenate([prev8, cur], axis=0)
    out = w[3:4, :] * cur
    for back in (1, 2, 3):
        out = out + w[3 - back:4 - back, :] * pltpu.roll(xw, back, 0)[V7X_SUBLANES:V7X_SUBLANES + rows, :]
    return out


def _dot(a, b):
    return jnp.dot(a, b, preferred_element_type=F32)


def _dot_split(a, b):
    a_hi = a.astype(BF16)
    b_hi = b.astype(BF16)
    a_lo = (a - a_hi.astype(F32)).astype(BF16)
    b_lo = (b - b_hi.astype(F32)).astype(BF16)
    stacked = _dot(jnp.concatenate([a_hi, a_lo], axis=1), jnp.concatenate([b_hi, b_hi], axis=0))
    return stacked + _dot(a_hi, b_lo)


def _dot_nt(a, b):
    return lax.dot_general(a, b, (((1,), (1,)), ((), ())), preferred_element_type=F32)


def _adaln_kernel(c_ref, w_ref, b_ref, o_ref):
    ca = _silu(c_ref[...])
    o_ref[...] = _dot(ca.astype(BF16), w_ref[...].astype(BF16)) + b_ref[...]


def _adaln(c_pad, w_ada, b_ada, tn):
    rows, d = c_pad.shape
    n = w_ada.shape[1]
    return pl.pallas_call(
        _adaln_kernel,
        grid=(n // tn,),
        in_specs=[pl.BlockSpec((rows, d), lambda j: (0, 0)),
                  pl.BlockSpec((d, tn), lambda j: (0, j)),
                  pl.BlockSpec((1, tn), lambda j: (0, j))],
        out_specs=pl.BlockSpec((rows, tn), lambda j: (0, j)),
        out_shape=jax.ShapeDtypeStruct((rows, n), F32),
        compiler_params=_params("arbitrary"),
        name="adaln",
    )(c_pad, w_ada, b_ada)


def _wprep_kernel(a_ref, b_ref, o_ref, *, n_aligned, skip):
    j = pl.program_id(1)

    @pl.when(j < n_aligned)
    def _():
        o_ref[...] = jnp.transpose(a_ref[...]).astype(o_ref.dtype)

    @pl.when(j >= n_aligned)
    def _():
        src = jnp.concatenate([a_ref[skip:, :], b_ref[...]], axis=0)
        o_ref[...] = jnp.transpose(src).astype(o_ref.dtype)


def _wprep(w_in_t, n_out, n_aligned_rows, skip, tk, tn):
    n_src, k = w_in_t.shape
    assert n_aligned_rows % tn == 0 and n_out % tn == 0 and tn % skip == 0 and skip % V7X_SUBLANES == 0
    assert n_src == n_out + skip
    kern = functools.partial(_wprep_kernel, n_aligned=n_aligned_rows // tn, skip=skip)
    return pl.pallas_call(
        kern,
        grid=(k // tk, n_out // tn),
        in_specs=[pl.BlockSpec((tn, tk), lambda i, j: (j, i)),
                  pl.BlockSpec((skip, tk), lambda i, j: ((j + 1) * (tn // skip), i))],
        out_specs=pl.BlockSpec((tk, tn), lambda i, j: (i, j)),
        out_shape=jax.ShapeDtypeStruct((k, n_out), BF16),
        compiler_params=_params("arbitrary", "arbitrary"),
        name="wprep",
    )(w_in_t, w_in_t)


def _prenorm_kernel(x_ref, mod_ref, nw_ref, wab_ref, h_ref, ab_ref):
    h = _rms(x_ref[...], nw_ref[...]) * (1.0 + mod_ref[1:2, :]) + mod_ref[0:1, :]
    hb = h.astype(BF16)
    h_ref[...] = hb
    ab_ref[...] = _dot(hb, wab_ref[...])


def _prenorm(x2, mod3, norm_w, w_ab, seq, tm):
    m, d = x2.shape
    nab = w_ab.shape[1]
    return pl.pallas_call(
        _prenorm_kernel,
        grid=(m // tm,),
        in_specs=[pl.BlockSpec((tm, d), lambda i: (i, 0)),
                  pl.BlockSpec((None, 6, d), lambda i: ((i * tm) // seq, 0, 0)),
                  pl.BlockSpec((1, d), lambda i: (0, 0)),
                  pl.BlockSpec((d, nab), lambda i: (0, 0))],
        out_specs=[pl.BlockSpec((tm, d), lambda i: (i, 0)),
                   pl.BlockSpec((tm, nab), lambda i: (i, 0))],
        out_shape=[jax.ShapeDtypeStruct((m, d), BF16),
                   jax.ShapeDtypeStruct((m, nab), F32)],
        compiler_params=_params("arbitrary"),
        name="prenorm",
    )(x2, mod3, norm_w, w_ab)


def _proj_kernel(*refs, act, conv, cast, l2_scale, seq, group):
    refs = list(refs)
    tail_scr = refs.pop() if conv else None
    if cast:
        co_ref = refs.pop()
    o_ref = refs.pop()
    if cast:
        ci_ref = refs.pop()
        co_ref[...] = ci_ref[...].astype(co_ref.dtype)
    if conv:
        h_ref, w_ref, cw_ref, cb_ref = refs
    else:
        h_ref, w_ref = refs
    tm = h_ref.shape[0]
    tn = w_ref.shape[1]
    panel = min(tn, V7X_MXU_COLS)
    j = pl.program_id(1)
    if conv:
        new_seq = (pl.program_id(0) * tm) % seq == 0
    for p0 in range(0, tn, panel):
        cols = slice(p0, p0 + panel)
        y = _dot(h_ref[...], w_ref[:, cols])
        if conv:
            prev = jnp.where(new_seq, 0.0, tail_scr[j, :, cols])
            tail_scr[j, :, cols] = y[tm - V7X_SUBLANES:, :]
            y = _causal_conv(prev, y, cw_ref[:, cols]) + cb_ref[:, cols]
        if act == "silu":
            y = _silu(y)
        elif act == "gelu":
            y = _gelu_tanh(y)
        if l2_scale is None:
            o_ref[:, cols] = y.astype(o_ref.dtype)
        else:
            for g0 in range(0, panel, group):
                yg = y[:, g0:g0 + group]
                yg = yg * (lax.rsqrt(jnp.sum(yg * yg, axis=-1, keepdims=True) + EPS) * l2_scale)
                o_ref[:, p0 + g0:p0 + g0 + group] = yg.astype(o_ref.dtype)


def _cast_job(src, grid):
    steps = grid[0] * grid[1]
    rows, cols = src.shape
    assert rows % steps == 0 and (rows // steps) % (2 * V7X_SUBLANES) == 0
    spec = pl.BlockSpec((rows // steps, cols), lambda i, j: (i * grid[1] + j, 0))
    return spec, jax.ShapeDtypeStruct((rows, cols), BF16)


def _proj(h, w_main, col_base, n_cols, seq, tm, tn, act=None, conv_w=None, conv_b=None, l2_scale=None,
          group=V7X_LANES, cast_src=None, name="proj"):
    m, d = h.shape
    assert col_base % tn == 0 and n_cols % tn == 0 and seq % tm == 0
    jb = col_base // tn
    conv = conv_w is not None
    cast = cast_src is not None
    grid = (m // tm, n_cols // tn)
    kern = functools.partial(_proj_kernel, act=act, conv=conv, cast=cast, l2_scale=l2_scale, seq=seq,
                             group=group)
    in_specs = [pl.BlockSpec((tm, d), lambda i, j: (i, 0)),
                pl.BlockSpec((d, tn), lambda i, j: (0, jb + j))]
    args = [h, w_main]
    out_specs = pl.BlockSpec((tm, tn), lambda i, j: (i, j))
    out_shape = jax.ShapeDtypeStruct((m, n_cols), BF16)
    scratch = []
    if conv:
        if conv_b is None:
            conv_b = jnp.zeros((1, n_cols), F32)
        in_specs += [pl.BlockSpec((CONV_W, tn), lambda i, j: (0, j)),
                     pl.BlockSpec((1, tn), lambda i, j: (0, j))]
        args += [conv_w, conv_b]
        scratch = [pltpu.VMEM((n_cols // tn, V7X_SUBLANES, tn), F32)]
    if cast:
        cast_spec, cast_shape = _cast_job(cast_src, grid)
        in_specs, args = in_specs + [cast_spec], args + [cast_src]
        out_specs, out_shape = [out_specs, cast_spec], [out_shape, cast_shape]
    return pl.pallas_call(
        kern,
        grid=grid,
        in_specs=in_specs,
        out_specs=out_specs,
        out_shape=out_shape,
        scratch_shapes=scratch,
        compiler_params=_params("arbitrary", "arbitrary"),
        name=name,
    )(*args)


def _lru_kernel(x_ref, gate_ref, wai_ref, bai_ref, lam_ref, o_ref, a_scr, u_scr, h_scr, *, groups, tb):
    blk = V7X_LANES
    nseg = V7X_SUBLANES
    seg = tb // nseg
    pitch = seg + LRU_SEG_PAD_ROWS

    @pl.when(pl.program_id(2) == 0)
    def _():
        h_scr[...] = jnp.zeros_like(h_scr)

    sp = _softplus(-lam_ref[...])
    for g in range(groups):
        sl = slice(g * blk, (g + 1) * blk)
        xg = x_ref[:, sl].astype(F32)
        ri = _dot(x_ref[:, sl], wai_ref[g]) + bai_ref[g]
        r = _sigmoid(ri[:, :blk])
        i = _sigmoid(ri[:, blk:])
        log_a = (-LRU_C) * r * sp[:, sl]
        a = jnp.exp(log_a)
        u = jnp.sqrt(-jnp.tanh(log_a) * (1.0 + a * a)) * (i * xg)
        for s in range(nseg):
            a_scr[g, s * pitch:s * pitch + seg, :] = a[s * seg:(s + 1) * seg, :]
            u_scr[g, s * pitch:s * pitch + seg, :] = u[s * seg:(s + 1) * seg, :]

    def step(i, carry):
        out = []
        for g in range(groups):
            p, h = carry[g]
            idx = pl.ds(i, nseg, stride=pitch)
            a = a_scr[g, idx, :]
            u = u_scr[g, idx, :]
            p = a * p
            h = a * h + u
            a_scr[g, idx, :] = p
            u_scr[g, idx, :] = h
            out.append((p, h))
        return tuple(out)

    init = tuple((jnp.ones((nseg, blk), F32), jnp.zeros((nseg, blk), F32)) for _ in range(groups))
    ends = lax.fori_loop(0, seg, step, init, unroll=8)

    for g in range(groups):
        sl = slice(g * blk, (g + 1) * blk)
        p_end, h_end = ends[g]
        carry = h_scr[0:1, sl]
        for s in range(nseg):
            rows = slice(s * pitch, s * pitch + seg)
            h = u_scr[g, rows, :] + a_scr[g, rows, :] * carry
            o_ref[s * seg:(s + 1) * seg, sl] = (h * gate_ref[s * seg:(s + 1) * seg, sl].astype(F32)).astype(o_ref.dtype)
            carry = p_end[s:s + 1, :] * carry + h_end[s:s + 1, :]
        h_scr[0:1, sl] = carry


def _lru(xa3, gate3, w_ai, b_ai, lam, groups, tb):
    b, t, lw = xa3.shape
    blk = V7X_LANES
    w = groups * blk
    pitch = tb // V7X_SUBLANES + LRU_SEG_PAD_ROWS
    kern = functools.partial(_lru_kernel, groups=groups, tb=tb)
    blk_spec = pl.BlockSpec((None, tb, w), lambda bi, ci, ti: (bi, ti, ci))
    seg_scr = pltpu.VMEM((groups, V7X_SUBLANES * pitch, blk), F32)
    return pl.pallas_call(
        kern,
        grid=(b, lw // w, t // tb),
        in_specs=[blk_spec, blk_spec,
                  pl.BlockSpec((groups, blk, 2 * blk), lambda bi, ci, ti: (ci, 0, 0)),
                  pl.BlockSpec((groups, 1, 2 * blk), lambda bi, ci, ti: (ci, 0, 0)),
                  pl.BlockSpec((1, w), lambda bi, ci, ti: (0, ci))],
        out_specs=blk_spec,
        out_shape=jax.ShapeDtypeStruct((b, t, lw), BF16),
        scratch_shapes=[seg_scr, seg_scr, pltpu.VMEM((V7X_SUBLANES, w), F32)],
        compiler_params=_params("arbitrary", "arbitrary", "arbitrary"),
        name="rglru",
    )(xa3, gate3, w_ai, b_ai, lam)


def _gdn_prep_kernel(ab_ref, alog_ref, dtb_ref, gcol_ref, grow_ref, *, heads, chunk):
    t = ab_ref.shape[0]
    ab = ab_ref[...]
    g = -jnp.exp(alog_ref[...]) * _softplus(ab + dtb_ref[...])
    lane = lax.broadcasted_iota(jnp.int32, ab.shape, 1)
    row = lax.broadcasted_iota(jnp.int32, ab.shape, 0)
    pos = row & (chunk - 1)
    s = 1
    while s < chunk:
        g = g + jnp.where(pos >= s, pltpu.roll(g, s, 0), 0.0)
        s *= 2
    gcol_ref[...] = jnp.where(lane < heads, g, _sigmoid(ab))
    for c in range(t // chunk):
        grow_ref[c] = jnp.transpose(g[c * chunk:(c + 1) * chunk, :])[0:2 * V7X_SUBLANES, :]


def _gdn_prep(ab3, alog_pad, dtb_pad, heads, chunk):
    b, t, lanes = ab3.shape
    assert chunk == lanes and heads <= 2 * V7X_SUBLANES
    kern = functools.partial(_gdn_prep_kernel, heads=heads, chunk=chunk)
    return pl.pallas_call(
        kern,
        grid=(b,),
        in_specs=[pl.BlockSpec((None, t, lanes), lambda bi: (bi, 0, 0)),
                  pl.BlockSpec((1, lanes), lambda bi: (0, 0)),
                  pl.BlockSpec((1, lanes), lambda bi: (0, 0))],
        out_specs=[pl.BlockSpec((None, t, lanes), lambda bi: (bi, 0, 0)),
                   pl.BlockSpec((None, t // chunk, 2 * V7X_SUBLANES, chunk), lambda bi: (bi, 0, 0, 0))],
        out_shape=[jax.ShapeDtypeStruct((b, t, lanes), F32),
                   jax.ShapeDtypeStruct((b, t // chunk, 2 * V7X_SUBLANES, chunk), F32)],
        compiler_params=_params("arbitrary"),
        name="gdn_prep",
    )(ab3, alog_pad, dtb_pad)


def _gdn_gates_kernel(q_ref, k_ref, v_ref, z_ref, gcol_ref, grow_ref, nw_ref, h_ref, wg_ref,
                      o_ref, gates_ref, s_scr, *, heads, nh, chunk, dk, n_hg, n_t):
    c_len = chunk
    tb = q_ref.shape[0]
    n_chunks = tb // c_len
    step = pl.program_id(0)
    head0 = ((step // n_t) % n_hg) * nh

    @pl.when(step % n_t == 0)
    def _():
        s_scr[...] = jnp.zeros_like(s_scr)

    panel = min(wg_ref.shape[1], V7X_MXU_COLS)
    panels = list(range(0, wg_ref.shape[1], panel))
    levels = c_len.bit_length() - 2
    n_slots = 4 + 2 * levels + n_chunks
    due = {(2 * i + 1) * n_slots // (2 * len(panels)): p0 for i, p0 in enumerate(panels)}
    assert len(due) == len(panels)
    slot = [0]

    def between():
        p0 = due.get(slot[0])
        slot[0] += 1
        if p0 is not None:
            cols = slice(p0, p0 + panel)
            gates_ref[:, cols] = _sigmoid(_dot(h_ref[...], wg_ref[:, cols])).astype(gates_ref.dtype)

    lane = lax.broadcasted_iota(jnp.int32, (c_len, V7X_LANES), 1)
    ri = lax.broadcasted_iota(jnp.int32, (c_len, c_len), 0)
    ci = lax.broadcasted_iota(jnp.int32, (c_len, c_len), 1)

    pairs = [(c, h) for c in range(n_chunks) for h in range(nh)]
    loc = {}
    for c, h in pairs:
        rows = slice(c * c_len, (c + 1) * c_len)
        cols = slice(h * dk, (h + 1) * dk)
        head = head0 + h
        gb = gcol_ref[rows, :]
        k_bf = k_ref[rows, cols]
        q = q_ref[rows, cols].astype(F32)
        k = k_bf.astype(F32)
        g_col = jnp.sum(jnp.where(lane == head, gb, 0.0), axis=-1, keepdims=True)
        b_col = jnp.sum(jnp.where(lane == head + heads, gb, 0.0), axis=-1, keepdims=True)
        g_row = grow_ref[c, pl.ds(head, 1), :]
        e_g = jnp.exp(g_col)
        g_last = g_col[c_len - 1:c_len, :]
        kb = k * b_col
        loc[c, h] = dict(
            decay=jnp.exp(jnp.where(ri >= ci, g_col - g_row, -jnp.inf)),
            kq_lhs=jnp.concatenate([kb.astype(BF16), q_ref[rows, cols]], axis=0), k_bf=k_bf,
            rhs=jnp.concatenate([v_ref[rows, cols].astype(F32) * b_col, kb * e_g], axis=1),
            qe=q * e_g, kd=k * jnp.exp(g_last - g_col), e_last=jnp.exp(g_last))
    between()

    for p in pairs:
        loc[p]["kq"] = _dot_nt(loc[p]["kq_lhs"], loc[p]["k_bf"])
    between()
    for p in pairs:
        d = loc[p]
        d["x"] = jnp.where(ri > ci, -(d["kq"][:c_len] * d["decay"]), 0.0)
        d["attn"] = d["kq"][c_len:] * d["decay"]
        d["pow"] = d["x"].astype(BF16)
        d["t_off"] = d["x"]
    for _ in range(levels):
        for p in pairs:
            loc[p]["pow"] = _dot(loc[p]["pow"], loc[p]["pow"]).astype(BF16)
        between()
        for p in pairs:
            d = loc[p]
            p_bf = jnp.where(ri == ci, 1.0, d["t_off"]).astype(BF16)
            d["t_off"] = d["t_off"] + _dot(p_bf, d["pow"])
        between()
    for p in pairs:
        d = loc[p]
        d["t_bf"] = d["t_off"].astype(BF16)
        d["sol"] = d["rhs"] + _dot(d["t_bf"], d["rhs"].astype(BF16))
    between()
    for p in pairs:
        d = loc[p]
        d["res"] = (d["rhs"] - d["sol"]) + _dot_split(d["x"], d["sol"])
    between()
    for p in pairs:
        d = loc[p]
        sol = d["sol"] + (d["res"] + _dot(d["t_bf"], d["res"].astype(BF16)))
        d["u"] = sol[:, :dk]
        d["lhs_s"] = jnp.concatenate([sol[:, dk:], d["qe"]], axis=0).astype(BF16)
        d["lhs_v"] = jnp.concatenate([d["attn"], jnp.transpose(d["kd"])], axis=0).astype(BF16)

    state = [s_scr[h] for h in range(nh)]
    for c in range(n_chunks):
        rows = slice(c * c_len, (c + 1) * c_len)
        r_s = [_dot(loc[c, h]["lhs_s"], state[h].astype(BF16)) for h in range(nh)]
        v_new = [loc[c, h]["u"] - r_s[h][:c_len] for h in range(nh)]
        r_v = [_dot(loc[c, h]["lhs_v"], v_new[h].astype(BF16)) for h in range(nh)]
        for h in range(nh):
            cols = slice(h * dk, (h + 1) * dk)
            state[h] = state[h] * loc[c, h]["e_last"] + r_v[h][c_len:]
            o = r_s[h][c_len:] + r_v[h][:c_len]
            o = _rms(o, nw_ref[...]) * z_ref[rows, cols].astype(F32)
            o_ref[rows, cols] = o.astype(o_ref.dtype)
        between()
    assert slot[0] == n_slots
    for h in range(nh):
        s_scr[h] = state[h]


def _gdn_gates(q3, k3, v3, z3, gcol, grow, norm_w, h2d, w_main, gate_col0, n_gate_cols,
               heads, dk, dv, chunk, nh, tb, tm):
    b, t, _ = q3.shape
    m, d = h2d.shape
    assert dk == V7X_LANES and dv == V7X_LANES and heads % nh == 0 and t % tb == 0 and tb % chunk == 0
    w = nh * dk
    n_hg, n_t = heads // nh, t // tb
    steps = b * n_hg * n_t
    assert (m // tm) * n_gate_cols % steps == 0
    tn = (m // tm) * n_gate_cols // steps
    assert tn % min(tn, V7X_MXU_COLS) == 0 and n_gate_cols % tn == 0 and gate_col0 % tn == 0
    n_j = n_gate_cols // tn
    jb = gate_col0 // tn
    kern = functools.partial(_gdn_gates_kernel, heads=heads, nh=nh, chunk=chunk, dk=dk, n_hg=n_hg, n_t=n_t)

    def gdn_idx(s):
        return s // (n_hg * n_t), s % n_t, (s // n_t) % n_hg

    col = pl.BlockSpec((None, tb, w), lambda s: gdn_idx(s))
    return pl.pallas_call(
        kern,
        grid=(steps,),
        in_specs=[col, col, col, col,
                  pl.BlockSpec((None, tb, V7X_LANES), lambda s: (gdn_idx(s)[0], gdn_idx(s)[1], 0)),
                  pl.BlockSpec((None, tb // chunk, 2 * V7X_SUBLANES, chunk),
                               lambda s: (gdn_idx(s)[0], gdn_idx(s)[1], 0, 0)),
                  pl.BlockSpec((1, dv), lambda s: (0, 0)),
                  pl.BlockSpec((tm, d), lambda s: (s // n_j, 0)),
                  pl.BlockSpec((d, tn), lambda s: (0, jb + s % n_j))],
        out_specs=[col, pl.BlockSpec((tm, tn), lambda s: (s // n_j, s % n_j))],
        out_shape=[jax.ShapeDtypeStruct((b, t, heads * dv), BF16),
                   jax.ShapeDtypeStruct((m, n_gate_cols), BF16)],
        scratch_shapes=[pltpu.VMEM((nh, dk, dv), F32)],
        compiler_params=_params("arbitrary"),
        name="gdn_gates",
    )(q3, k3, v3, z3, gcol, grow, norm_w, h2d, w_main)


def _merge_kernel(ya_ref, yb_ref, wl_ref, wg_ref, gl_ref, gg_ref, ci_ref, o_ref, co_ref):
    pa = _dot(ya_ref[...], wl_ref[...])
    pb = _dot(yb_ref[...], wg_ref[...])
    o_ref[...] = (gl_ref[...].astype(F32) * pa + gg_ref[...].astype(F32) * pb).astype(o_ref.dtype)
    co_ref[...] = ci_ref[...].astype(co_ref.dtype)


def _merge(ya2, yb2, w_l, w_g, gates, cast_src, tm, tn):
    m, ka = ya2.shape
    kb = yb2.shape[1]
    d = w_l.shape[1]
    assert d % tn == 0
    gg0 = d // tn
    grid = (m // tm, d // tn)
    cast_spec, cast_shape = _cast_job(cast_src, grid)
    return pl.pallas_call(
        _merge_kernel,
        grid=grid,
        in_specs=[pl.BlockSpec((tm, ka), lambda i, j: (i, 0)),
                  pl.BlockSpec((tm, kb), lambda i, j: (i, 0)),
                  pl.BlockSpec((ka, tn), lambda i, j: (0, j)),
                  pl.BlockSpec((kb, tn), lambda i, j: (0, j)),
                  pl.BlockSpec((tm, tn), lambda i, j: (i, j)),
                  pl.BlockSpec((tm, tn), lambda i, j: (i, gg0 + j)),
                  cast_spec],
        out_specs=[pl.BlockSpec((tm, tn), lambda i, j: (i, j)), cast_spec],
        out_shape=[jax.ShapeDtypeStruct((m, d), BF16), cast_shape],
        compiler_params=_params("arbitrary", "arbitrary"),
        name="merge",
    )(ya2, yb2, w_l, w_g, gates, gates, cast_src)


def _mm_kernel(*refs, relu2, cast):
    if cast:
        a_ref, w_ref, ci_ref, o_ref, co_ref = refs
        co_ref[...] = ci_ref[...].astype(co_ref.dtype)
    else:
        a_ref, w_ref, o_ref = refs
    y = _dot(a_ref[...], w_ref[...])
    if relu2:
        y = jnp.square(jnp.maximum(y, 0.0))
    o_ref[...] = y.astype(o_ref.dtype)


def _mm(a, w, out_dtype, tm, tn, relu2=False, cast_src=None, name="mm"):
    m, k = a.shape
    n = w.shape[1]
    grid = (m // tm, n // tn)
    in_specs = [pl.BlockSpec((tm, k), lambda i, j: (i, 0)),
                pl.BlockSpec((k, tn), lambda i, j: (0, j))]
    out_specs = pl.BlockSpec((tm, tn), lambda i, j: (i, j))
    out_shape = jax.ShapeDtypeStruct((m, n), out_dtype)
    args = (a, w)
    if cast_src is not None:
        cast_spec, cast_shape = _cast_job(cast_src, grid)
        in_specs, args = in_specs + [cast_spec], args + (cast_src,)
        out_specs, out_shape = [out_specs, cast_spec], [out_shape, cast_shape]
    return pl.pallas_call(
        functools.partial(_mm_kernel, relu2=relu2, cast=cast_src is not None),
        grid=grid,
        in_specs=in_specs,
        out_specs=out_specs,
        out_shape=out_shape,
        compiler_params=_params("arbitrary", "arbitrary"),
        name=name,
    )(*args)


def _mm_acc_kernel(a_ref, w_ref, o_ref, acc_ref):
    kk = pl.program_id(2)

    @pl.when(kk == 0)
    def _():
        acc_ref[...] = jnp.zeros_like(acc_ref)

    acc_ref[...] += _dot(a_ref[...], w_ref[...])

    @pl.when(kk == pl.num_programs(2) - 1)
    def _():
        o_ref[...] = acc_ref[...].astype(o_ref.dtype)


def _mm_acc(a, w, out_dtype, tm, tn, tk, name="mm_acc"):
    m, k = a.shape
    n = w.shape[1]
    return pl.pallas_call(
        _mm_acc_kernel,
        grid=(m // tm, n // tn, k // tk),
        in_specs=[pl.BlockSpec((tm, tk), lambda i, j, kk: (i, kk)),
                  pl.BlockSpec((tk, tn), lambda i, j, kk: (kk, j))],
        out_specs=pl.BlockSpec((tm, tn), lambda i, j, kk: (i, j)),
        out_shape=jax.ShapeDtypeStruct((m, n), out_dtype),
        scratch_shapes=[pltpu.VMEM((tm, tn), F32)],
        compiler_params=_params("arbitrary", "arbitrary", "arbitrary"),
        name=name,
    )(a, w)


def _resid_prenorm_kernel(y_ref, x_ref, mod_ref, pw_ref, nw_ref, x1_ref, h_ref):
    x1 = x_ref[...] + mod_ref[2:3, :] * _rms(y_ref[...].astype(F32), pw_ref[...])
    x1_ref[...] = x1
    h = _rms(x1, nw_ref[...]) * (1.0 + mod_ref[4:5, :]) + mod_ref[3:4, :]
    h_ref[...] = h.astype(h_ref.dtype)


def _resid_prenorm(y, x2, mod3, post_w, pre_w, seq, tm):
    m, d = x2.shape
    row = pl.BlockSpec((tm, d), lambda i: (i, 0))
    vec = pl.BlockSpec((1, d), lambda i: (0, 0))
    return pl.pallas_call(
        _resid_prenorm_kernel,
        grid=(m // tm,),
        in_specs=[row, row, pl.BlockSpec((None, 6, d), lambda i: ((i * tm) // seq, 0, 0)), vec, vec],
        out_specs=[row, row],
        out_shape=[jax.ShapeDtypeStruct((m, d), F32), jax.ShapeDtypeStruct((m, d), BF16)],
        compiler_params=_params("arbitrary"),
        name="resid_prenorm",
    )(y, x2, mod3, post_w, pre_w)


def _resid_kernel(y_ref, x_ref, mod_ref, pw_ref, o_ref):
    o_ref[...] = x_ref[...] + mod_ref[5:6, :] * _rms(y_ref[...].astype(F32), pw_ref[...])


def _resid(y, x1, mod3, post_w, seq, tm):
    m, d = x1.shape
    row = pl.BlockSpec((tm, d), lambda i: (i, 0))
    return pl.pallas_call(
        _resid_kernel,
        grid=(m // tm,),
        in_specs=[row, row, pl.BlockSpec((None, 6, d), lambda i: ((i * tm) // seq, 0, 0)),
                  pl.BlockSpec((1, d), lambda i: (0, 0))],
        out_specs=row,
        out_shape=jax.ShapeDtypeStruct((m, d), F32),
        compiler_params=_params("arbitrary"),
        name="resid",
    )(y, x1, mod3, post_w)


def _tile(n, pref):
    t = min(n, pref)
    while n % t:
        t //= 2
    return t


def _layer(x2, mod3, batch, seq, mix_pre_norm, mix_post_norm, w_in, lru_conv_w, lru_conv_b,
           lru_gate_a_w, lru_gate_a_b, lru_gate_i_w, lru_gate_i_b, lru_lambda, gdn_conv_w,
           gdn_a_log, gdn_dt_bias, gdn_out_norm, w_branch_lru, w_branch_gdn, w_out,
           mlp_pre_norm, mlp_post_norm, w_mlp_up, w_mlp_down):
    m, d = x2.shape
    lw = lru_lambda.shape[-1]
    nblk = lru_gate_a_w.shape[0]
    heads = gdn_a_log.shape[-1]
    dv = gdn_out_norm.shape[-1]
    val = heads * dv
    conv_dim = gdn_conv_w.shape[-1]
    key = (conv_dim - val) // 2
    dk = key // heads
    lanes = V7X_LANES

    o_q = 2 * lw
    o_z = o_q + conv_dim
    o_a = o_z + val
    n_main = o_a + 2 * d
    w_in_t = jnp.swapaxes(w_in, 0, 1)
    w_main = _wprep(w_in_t, n_main, o_a, 2 * heads, _tile(d, 1024), _tile(o_a, 1024))
    w_ab = jnp.pad(w_in[:, o_a:o_a + 2 * heads], ((0, 0), (0, lanes - 2 * heads))).astype(BF16)

    h, ab = _prenorm(x2, mod3, mix_pre_norm[None, :], w_ab, seq, _tile(seq, 512))

    tm = _tile(seq, 1024)
    tn = _tile(min(lw, key, val), 1024)
    proj = functools.partial(_proj, h, w_main, seq=seq, tm=tm, tn=tn)
    xa = proj(0, lw, conv_w=lru_conv_w, conv_b=lru_conv_b[None, :], name="proj_lru_x")
    gel, w_out_bf = proj(lw, lw, act="gelu", cast_src=w_out, name="proj_lru_gate")
    gq = proj(o_q, key, act="silu", conv_w=gdn_conv_w[:, :key], l2_scale=dk ** -0.5, group=dk, name="proj_q")
    gk = proj(o_q + key, key, act="silu", conv_w=gdn_conv_w[:, key:2 * key], l2_scale=1.0, group=dk, name="proj_k")
    gv, w_bg_bf = proj(o_q + 2 * key, val, act="silu", conv_w=gdn_conv_w[:, 2 * key:], cast_src=w_branch_gdn,
                       name="proj_v")
    gz, w_bl_bf = proj(o_z, val, act="silu", cast_src=w_branch_lru, name="proj_z")

    groups = 4 if nblk % 4 == 0 else 1
    w_ai = jnp.concatenate([lru_gate_a_w, lru_gate_i_w], axis=-1).astype(BF16)
    b_ai = jnp.concatenate([lru_gate_a_b, lru_gate_i_b], axis=-1)[:, None, :]
    as3 = lambda a: a.reshape(batch, seq, a.shape[-1])
    ya = _lru(as3(xa), as3(gel), w_ai, b_ai, lru_lambda[None, :], groups, _tile(seq, LRU_ROWS_PER_STEP))

    pad_h = lambda v: jnp.pad(v, (0, lanes - heads))[None, :]
    gcol, grow = _gdn_prep(as3(ab), pad_h(gdn_a_log), pad_h(gdn_dt_bias), heads, GDN_CHUNK)
    nh = GDN_HEADS_PER_STEP if heads % GDN_HEADS_PER_STEP == 0 else 1
    yb, gates = _gdn_gates(as3(gq), as3(gk), as3(gv), as3(gz), gcol, grow, gdn_out_norm[None, :],
                           h, w_main, o_a, 2 * d, heads, dk, dv, GDN_CHUNK, nh,
                           _tile(seq, GDN_ROWS_PER_STEP), tm)

    tm_mid = _tile(m, 512)
    merged, w_up_bf = _merge(ya.reshape(m, lw), yb.reshape(m, val), w_bl_bf, w_bg_bf, gates, w_mlp_up,
                             tm_mid, _tile(d, 1024))
    tm_big = _tile(m, 1024)
    y1, w_down_bf = _mm(merged, w_out_bf, BF16, tm_big, _tile(d, 1024), cast_src=w_mlp_down, name="out_proj")
    x1, h2 = _resid_prenorm(y1, x2, mod3, mix_post_norm[None, :], mlp_pre_norm[None, :], seq, _tile(m, 256))

    d_ff = w_mlp_up.shape[1]
    hid = _mm(h2, w_up_bf, BF16, tm_big, _tile(d_ff, 1024), relu2=True, name="mlp_up")
    y2 = _mm_acc(hid, w_down_bf, BF16, tm_big, _tile(d, 1024), _tile(d_ff, 4096), name="mlp_down")
    return _resid(y2, x1, mod3, mlp_post_norm[None, :], seq, _tile(seq, 512))


def kernel(x, c, w_ada, b_ada, mix_pre_norm, mix_post_norm, w_in, lru_conv_w, lru_conv_b, lru_gate_a_w, lru_gate_a_b, lru_gate_i_w, lru_gate_i_b, lru_lambda, gdn_conv_w, gdn_a_log, gdn_dt_bias, gdn_out_norm, w_branch_lru, w_branch_gdn, w_out, mlp_pre_norm, mlp_post_norm, w_mlp_up, w_mlp_down):
    batch, seq, d = x.shape
    depth = w_ada.shape[0]
    x2 = x.reshape(batch * seq, d)
    c_pad = jnp.pad(c, ((0, V7X_SUBLANES - batch % V7X_SUBLANES), (0, 0))) if batch % V7X_SUBLANES else c
    for l in range(depth):
        mod = _adaln(c_pad, w_ada[l], b_ada[l][None, :], _tile(6 * d, 512))
        mod3 = mod[:batch].reshape(batch, 6, d)
        x2 = _layer(x2, mod3, batch, seq, mix_pre_norm[l], mix_post_norm[l], w_in[l], lru_conv_w[l],
                    lru_conv_b[l], lru_gate_a_w[l], lru_gate_a_b[l], lru_gate_i_w[l], lru_gate_i_b[l],
                    lru_lambda[l], gdn_conv_w[l], gdn_a_log[l], gdn_dt_bias[l], gdn_out_norm[l],
                    w_branch_lru[l], w_branch_gdn[l], w_out[l], mlp_pre_norm[l], mlp_post_norm[l],
                    w_mlp_up[l], w_mlp_down[l])
    return x2.reshape(batch, seq, d)
```

```python
import functools

import jax
import jax.numpy as jnp
from jax import lax
from jax.experimental import pallas as pl
from jax.experimental.pallas import tpu as pltpu

F32 = jnp.float32
BF16 = jnp.bfloat16

EPS = 1e-6
CONV_W = 4
LRU_C = 8.0
V7X_LANES = 128
V7X_SUBLANES = 8
V7X_MXU_COLS = 256
V7X_VMEM_LIMIT_BYTES = 56 * 1024 * 1024
GDN_CHUNK = 128
GDN_HEADS_PER_STEP = 4
GDN_ROWS_PER_STEP = 512
LRU_ROWS_PER_STEP = 512
LRU_SEG_PAD_ROWS = 8


def _params(*sem):
    return pltpu.CompilerParams(dimension_semantics=sem, vmem_limit_bytes=V7X_VMEM_LIMIT_BYTES)


def _sigmoid(x):
    return 1.0 / (1.0 + jnp.exp(-x))


def _silu(x):
    return x * _sigmoid(x)


def _softplus(x):
    return jnp.maximum(x, 0.0) + jnp.log1p(jnp.exp(-jnp.abs(x)))


def _gelu_tanh(x):
    c = 0.7978845608028654
    return 0.5 * x * (1.0 + jnp.tanh(c * (x + 0.044715 * (x * x * x))))


def _rms(x, w):
    return x * lax.rsqrt(jnp.mean(x * x, axis=-1, keepdims=True) + EPS) * w


def _causal_conv(prev8, cur, w):
    rows = cur.shape[0]
    xw = jnp.concatenate([prev8, cur], axis=0)
    out = w[3:4, :] * cur
    for back in (1, 2, 3):
        out = out + w[3 - back:4 - back, :] * pltpu.roll(xw, back, 0)[V7X_SUBLANES:V7X_SUBLANES + rows, :]
    return out


def _dot(a, b):
    return jnp.dot(a, b, preferred_element_type=F32)


def _dot_split(a, b):
    a_hi = a.astype(BF16)
    b_hi = b.astype(BF16)
    a_lo = (a - a_hi.astype(F32)).astype(BF16)
    b_lo = (b - b_hi.astype(F32)).astype(BF16)
    stacked = _dot(jnp.concatenate([a_hi, a_lo], axis=1), jnp.concatenate([b_hi, b_hi], axis=0))
    return stacked + _dot(a_hi, b_lo)


def _dot_nt(a, b):
    return lax.dot_general(a, b, (((1,), (1,)), ((), ())), preferred_element_type=F32)


def _adaln_kernel(c_ref, w_ref, b_ref, o_ref):
    ca = _silu(c_ref[...])
    o_ref[...] = _dot(ca.astype(BF16), w_ref[...].astype(BF16)) + b_ref[...]


def _adaln(c_pad, w_ada, b_ada, tn):
    rows, d = c_pad.shape
    n = w_ada.shape[1]
    return pl.pallas_call(
        _adaln_kernel,
        grid=(n // tn,),
        in_specs=[pl.BlockSpec((rows, d), lambda j: (0, 0)),
                  pl.BlockSpec((d, tn), lambda j: (0, j)),
                  pl.BlockSpec((1, tn), lambda j: (0, j))],
        out_specs=pl.BlockSpec((rows, tn), lambda j: (0, j)),
        out_shape=jax.ShapeDtypeStruct((rows, n), F32),
        compiler_params=_params("arbitrary"),
        name="adaln",
    )(c_pad, w_ada, b_ada)


def _wprep_kernel(a_ref, b_ref, o_ref, *, n_aligned, skip):
    j = pl.program_id(1)

    @pl.when(j < n_aligned)
    def _():
        o_ref[...] = jnp.transpose(a_ref[...]).astype(o_ref.dtype)

    @pl.when(j >= n_aligned)
    def _():
        src = jnp.concatenate([a_ref[skip:, :], b_ref[...]], axis=0)
        o_ref[...] = jnp.transpose(src).astype(o_ref.dtype)


def _wprep(w_in_t, n_out, n_aligned_rows, skip, tk, tn):
    n_src, k = w_in_t.shape
    assert n_aligned_rows % tn == 0 and n_out % tn == 0 and tn % skip == 0 and skip % V7X_SUBLANES == 0
    assert n_src == n_out + skip
    kern = functools.partial(_wprep_kernel, n_aligned=n_aligned_rows // tn, skip=skip)
    return pl.pallas_call(
        kern,
        grid=(k // tk, n_out // tn),
        in_specs=[pl.BlockSpec((tn, tk), lambda i, j: (j, i)),
                  pl.BlockSpec((skip, tk), lambda i, j: ((j + 1) * (tn // skip), i))],
        out_specs=pl.BlockSpec((tk, tn), lambda i, j: (i, j)),
        out_shape=jax.ShapeDtypeStruct((k, n_out), BF16),
        compiler_params=_params("arbitrary", "arbitrary"),
        name="wprep",
    )(w_in_t, w_in_t)


def _prenorm_kernel(x_ref, mod_ref, nw_ref, wab_ref, h_ref, ab_ref):
    h = _rms(x_ref[...], nw_ref[...]) * (1.0 + mod_ref[1:2, :]) + mod_ref[0:1, :]
    hb = h.astype(BF16)
    h_ref[...] = hb
    ab_ref[...] = _dot(hb, wab_ref[...])


def _prenorm(x2, mod3, norm_w, w_ab, seq, tm):
    m, d = x2.shape
    nab = w_ab.shape[1]
    return pl.pallas_call(
        _prenorm_kernel,
        grid=(m // tm,),
        in_specs=[pl.BlockSpec((tm, d), lambda i: (i, 0)),
                  pl.BlockSpec((None, 6, d), lambda i: ((i * tm) // seq, 0, 0)),
                  pl.BlockSpec((1, d), lambda i: (0, 0)),
                  pl.BlockSpec((d, nab), lambda i: (0, 0))],
        out_specs=[pl.BlockSpec((tm, d), lambda i: (i, 0)),
                   pl.BlockSpec((tm, nab), lambda i: (i, 0))],
        out_shape=[jax.ShapeDtypeStruct((m, d), BF16),
                   jax.ShapeDtypeStruct((m, nab), F32)],
        compiler_params=_params("arbitrary"),
        name="prenorm",
    )(x2, mod3, norm_w, w_ab)


def _proj_kernel(*refs, act, conv, cast, l2_scale, seq, group):
    refs = list(refs)
    tail_scr = refs.pop() if conv else None
    if cast:
        co_ref = refs.pop()
    o_ref = refs.pop()
    if cast:
        ci_ref = refs.pop()
        co_ref[...] = ci_ref[...].astype(co_ref.dtype)
    if conv:
        h_ref, w_ref, cw_ref, cb_ref = refs
    else:
        h_ref, w_ref = refs
    tm = h_ref.shape[0]
    tn = w_ref.shape[1]
    panel = min(tn, V7X_MXU_COLS)
    j = pl.program_id(1)
    if conv:
        new_seq = (pl.program_id(0) * tm) % seq == 0
    for p0 in range(0, tn, panel):
        cols = slice(p0, p0 + panel)
        y = _dot(h_ref[...], w_ref[:, cols])
        if conv:
            prev = jnp.where(new_seq, 0.0, tail_scr[j, :, cols])
            tail_scr[j, :, cols] = y[tm - V7X_SUBLANES:, :]
            y = _causal_conv(prev, y, cw_ref[:, cols]) + cb_ref[:, cols]
        if act == "silu":
            y = _silu(y)
        elif act == "gelu":
            y = _gelu_tanh(y)
        if l2_scale is None:
            o_ref[:, cols] = y.astype(o_ref.dtype)
        else:
            for g0 in range(0, panel, group):
                yg = y[:, g0:g0 + group]
                yg = yg * (lax.rsqrt(jnp.sum(yg * yg, axis=-1, keepdims=True) + EPS) * l2_scale)
                o_ref[:, p0 + g0:p0 + g0 + group] = yg.astype(o_ref.dtype)


def _cast_job(src, grid):
    steps = grid[0] * grid[1]
    rows, cols = src.shape
    assert rows % steps == 0 and (rows // steps) % (2 * V7X_SUBLANES) == 0
    spec = pl.BlockSpec((rows // steps, cols), lambda i, j: (i * grid[1] + j, 0))
    return spec, jax.ShapeDtypeStruct((rows, cols), BF16)


def _proj(h, w_main, col_base, n_cols, seq, tm, tn, act=None, conv_w=None, conv_b=None, l2_scale=None,
          group=V7X_LANES, cast_src=None, name="proj"):
    m, d = h.shape
    assert col_base % tn == 0 and n_cols % tn == 0 and seq % tm == 0
    jb = col_base // tn
    conv = conv_w is not None
    cast = cast_src is not None
    grid = (m // tm, n_cols // tn)
    kern = functools.partial(_proj_kernel, act=act, conv=conv, cast=cast, l2_scale=l2_scale, seq=seq,
                             group=group)
    in_specs = [pl.BlockSpec((tm, d), lambda i, j: (i, 0)),
                pl.BlockSpec((d, tn), lambda i, j: (0, jb + j))]
    args = [h, w_main]
    out_specs = pl.BlockSpec((tm, tn), lambda i, j: (i, j))
    out_shape = jax.ShapeDtypeStruct((m, n_cols), BF16)
    scratch = []
    if conv:
        if conv_b is None:
            conv_b = jnp.zeros((1, n_cols), F32)
        in_specs += [pl.BlockSpec((CONV_W, tn), lambda i, j: (0, j)),
                     pl.BlockSpec((1, tn), lambda i, j: (0, j))]
        args += [conv_w, conv_b]
        scratch = [pltpu.VMEM((n_cols // tn, V7X_SUBLANES, tn), F32)]
    if cast:
        cast_spec, cast_shape = _cast_job(cast_src, grid)
        in_specs, args = in_specs + [cast_spec], args + [cast_src]
        out_specs, out_shape = [out_specs, cast_spec], [out_shape, cast_shape]
    return pl.pallas_call(
        kern,
        grid=grid,
        in_specs=in_specs,
        out_specs=out_specs,
        out_shape=out_shape,
        scratch_shapes=scratch,
        compiler_params=_params("arbitrary", "arbitrary"),
        name=name,
    )(*args)


def _lru_kernel(x_ref, gate_ref, wai_ref, bai_ref, lam_ref, o_ref, a_scr, u_scr, h_scr, *, groups, tb):
    blk = V7X_LANES
    nseg = V7X_SUBLANES
    seg = tb // nseg
    pitch = seg + LRU_SEG_PAD_ROWS

    @pl.when(pl.program_id(2) == 0)
    def _():
        h_scr[...] = jnp.zeros_like(h_scr)

    sp = _softplus(-lam_ref[...])
    for g in range(groups):
        sl = slice(g * blk, (g + 1) * blk)
        xg = x_ref[:, sl].astype(F32)
        ri = _dot(x_ref[:, sl], wai_ref[g]) + bai_ref[g]
        r = _sigmoid(ri[:, :blk])
        i = _sigmoid(ri[:, blk:])
        log_a = (-LRU_C) * r * sp[:, sl]
        a = jnp.exp(log_a)
        u = jnp.sqrt(-jnp.tanh(log_a) * (1.0 + a * a)) * (i * xg)
        for s in range(nseg):
            a_scr[g, s * pitch:s * pitch + seg, :] = a[s * seg:(s + 1) * seg, :]
            u_scr[g, s * pitch:s * pitch + seg, :] = u[s * seg:(s + 1) * seg, :]

    def step(i, carry):
        out = []
        for g in range(groups):
            p, h = carry[g]
            idx = pl.ds(i, nseg, stride=pitch)
            a = a_scr[g, idx, :]
            u = u_scr[g, idx, :]
            p = a * p
            h = a * h + u
            a_scr[g, idx, :] = p
            u_scr[g, idx, :] = h
            out.append((p, h))
        return tuple(out)

    init = tuple((jnp.ones((nseg, blk), F32), jnp.zeros((nseg, blk), F32)) for _ in range(groups))
    ends = lax.fori_loop(0, seg, step, init, unroll=8)

    for g in range(groups):
        sl = slice(g * blk, (g + 1) * blk)
        p_end, h_end = ends[g]
        carry = h_scr[0:1, sl]
        for s in range(nseg):
            rows = slice(s * pitch, s * pitch + seg)
            h = u_scr[g, rows, :] + a_scr[g, rows, :] * carry
            o_ref[s * seg:(s + 1) * seg, sl] = (h * gate_ref[s * seg:(s + 1) * seg, sl].astype(F32)).astype(o_ref.dtype)
            carry = p_end[s:s + 1, :] * carry + h_end[s:s + 1, :]
        h_scr[0:1, sl] = carry


def _lru(xa3, gate3, w_ai, b_ai, lam, groups, tb):
    b, t, lw = xa3.shape
    blk = V7X_LANES
    w = groups * blk
    pitch = tb // V7X_SUBLANES + LRU_SEG_PAD_ROWS
    kern = functools.partial(_lru_kernel, groups=groups, tb=tb)
    blk_spec = pl.BlockSpec((None, tb, w), lambda bi, ci, ti: (bi, ti, ci))
    seg_scr = pltpu.VMEM((groups, V7X_SUBLANES * pitch, blk), F32)
    return pl.pallas_call(
        kern,
        grid=(b, lw // w, t // tb),
        in_specs=[blk_spec, blk_spec,
                  pl.BlockSpec((groups, blk, 2 * blk), lambda bi, ci, ti: (ci, 0, 0)),
                  pl.BlockSpec((groups, 1, 2 * blk), lambda bi, ci, ti: (ci, 0, 0)),
                  pl.BlockSpec((1, w), lambda bi, ci, ti: (0, ci))],
        out_specs=blk_spec,
        out_shape=jax.ShapeDtypeStruct((b, t, lw), BF16),
        scratch_shapes=[seg_scr, seg_scr, pltpu.VMEM((V7X_SUBLANES, w), F32)],
        compiler_params=_params("arbitrary", "arbitrary", "arbitrary"),
        name="rglru",
    )(xa3, gate3, w_ai, b_ai, lam)


def _gdn_prep_kernel(ab_ref, alog_ref, dtb_ref, gcol_ref, grow_ref, *, heads, chunk):
    t = ab_ref.shape[0]
    ab = ab_ref[...]
    g = -jnp.exp(alog_ref[...]) * _softplus(ab + dtb_ref[...])
    lane = lax.broadcasted_iota(jnp.int32, ab.shape, 1)
    row = lax.broadcasted_iota(jnp.int32, ab.shape, 0)
    pos = row & (chunk - 1)
    s = 1
    while s < chunk:
        g = g + jnp.where(pos >= s, pltpu.roll(g, s, 0), 0.0)
        s *= 2
    gcol_ref[...] = jnp.where(lane < heads, g, _sigmoid(ab))
    for c in range(t // chunk):
        grow_ref[c] = jnp.transpose(g[c * chunk:(c + 1) * chunk, :])[0:2 * V7X_SUBLANES, :]


def _gdn_prep(ab3, alog_pad, dtb_pad, heads, chunk):
    b, t, lanes = ab3.shape
    assert chunk == lanes and heads <= 2 * V7X_SUBLANES
    kern = functools.partial(_gdn_prep_kernel, heads=heads, chunk=chunk)
    return pl.pallas_call(
        kern,
        grid=(b,),
        in_specs=[pl.BlockSpec((None, t, lanes), lambda bi: (bi, 0, 0)),
                  pl.BlockSpec((1, lanes), lambda bi: (0, 0)),
                  pl.BlockSpec((1, lanes), lambda bi: (0, 0))],
        out_specs=[pl.BlockSpec((None, t, lanes), lambda bi: (bi, 0, 0)),
                   pl.BlockSpec((None, t // chunk, 2 * V7X_SUBLANES, chunk), lambda bi: (bi, 0, 0, 0))],
        out_shape=[jax.ShapeDtypeStruct((b, t, lanes), F32),
                   jax.ShapeDtypeStruct((b, t // chunk, 2 * V7X_SUBLANES, chunk), F32)],
        compiler_params=_params("arbitrary"),
        name="gdn_prep",
    )(ab3, alog_pad, dtb_pad)


def _gdn_gates_kernel(q_ref, k_ref, v_ref, z_ref, gcol_ref, grow_ref, nw_ref, h_ref, wg_ref,
                      o_ref, gates_ref, s_scr, *, heads, nh, chunk, dk, n_hg, n_t):
    c_len = chunk
    tb = q_ref.shape[0]
    n_chunks = tb // c_len
    step = pl.program_id(0)
    head0 = ((step // n_t) % n_hg) * nh

    @pl.when(step % n_t == 0)
    def _():
        s_scr[...] = jnp.zeros_like(s_scr)

    panel = min(wg_ref.shape[1], V7X_MXU_COLS)
    panels = list(range(0, wg_ref.shape[1], panel))
    levels = c_len.bit_length() - 2
    n_slots = 4 + 2 * levels + n_chunks
    due = {(2 * i + 1) * n_slots // (2 * len(panels)): p0 for i, p0 in enumerate(panels)}
    assert len(due) == len(panels)
    slot = [0]

    def between():
        p0 = due.get(slot[0])
        slot[0] += 1
        if p0 is not None:
            cols = slice(p0, p0 + panel)
            gates_ref[:, cols] = _sigmoid(_dot(h_ref[...], wg_ref[:, cols])).astype(gates_ref.dtype)

    lane = lax.broadcasted_iota(jnp.int32, (c_len, V7X_LANES), 1)
    ri = lax.broadcasted_iota(jnp.int32, (c_len, c_len), 0)
    ci = lax.broadcasted_iota(jnp.int32, (c_len, c_len), 1)

    pairs = [(c, h) for c in range(n_chunks) for h in range(nh)]
    loc = {}
    for c, h in pairs:
        rows = slice(c * c_len, (c + 1) * c_len)
        cols = slice(h * dk, (h + 1) * dk)
        head = head0 + h
        gb = gcol_ref[rows, :]
        k_bf = k_ref[rows, cols]
        q = q_ref[rows, cols].astype(F32)
        k = k_bf.astype(F32)
        g_col = jnp.sum(jnp.where(lane == head, gb, 0.0), axis=-1, keepdims=True)
        b_col = jnp.sum(jnp.where(lane == head + heads, gb, 0.0), axis=-1, keepdims=True)
        g_row = grow_ref[c, pl.ds(head, 1), :]
        e_g = jnp.exp(g_col)
        g_last = g_col[c_len - 1:c_len, :]
        kb = k * b_col
        loc[c, h] = dict(
            decay=jnp.exp(jnp.where(ri >= ci, g_col - g_row, -jnp.inf)),
            kq_lhs=jnp.concatenate([kb.astype(BF16), q_ref[rows, cols]], axis=0), k_bf=k_bf,
            rhs=jnp.concatenate([v_ref[rows, cols].astype(F32) * b_col, kb * e_g], axis=1),
            qe=q * e_g, kd=k * jnp.exp(g_last - g_col), e_last=jnp.exp(g_last))
    between()

    for p in pairs:
        loc[p]["kq"] = _dot_nt(loc[p]["kq_lhs"], loc[p]["k_bf"])
    between()
    for p in pairs:
        d = loc[p]
        d["x"] = jnp.where(ri > ci, -(d["kq"][:c_len] * d["decay"]), 0.0)
        d["attn"] = d["kq"][c_len:] * d["decay"]
        d["pow"] = d["x"].astype(BF16)
        d["t_off"] = d["x"]
    for _ in range(levels):
        for p in pairs:
            loc[p]["pow"] = _dot(loc[p]["pow"], loc[p]["pow"]).astype(BF16)
        between()
        for p in pairs:
            d = loc[p]
            p_bf = jnp.where(ri == ci, 1.0, d["t_off"]).astype(BF16)
            d["t_off"] = d["t_off"] + _dot(p_bf, d["pow"])
        between()
    for p in pairs:
        d = loc[p]
        d["t_bf"] = d["t_off"].astype(BF16)
        d["sol"] = d["rhs"] + _dot(d["t_bf"], d["rhs"].astype(BF16))
    between()
    for p in pairs:
        d = loc[p]
        d["res"] = (d["rhs"] - d["sol"]) + _dot_split(d["x"], d["sol"])
    between()
    for p in pairs:
        d = loc[p]
        sol = d["sol"] + (d["res"] + _dot(d["t_bf"], d["res"].astype(BF16)))
        d["u"] = sol[:, :dk]
        d["lhs_s"] = jnp.concatenate([sol[:, dk:], d["qe"]], axis=0).astype(BF16)
        d["lhs_v"] = jnp.concatenate([d["attn"], jnp.transpose(d["kd"])], axis=0).astype(BF16)

    state = [s_scr[h] for h in range(nh)]
    for c in range(n_chunks):
        rows = slice(c * c_len, (c + 1) * c_len)
        r_s = [_dot(loc[c, h]["lhs_s"], state[h].astype(BF16)) for h in range(nh)]
        v_new = [loc[c, h]["u"] - r_s[h][:c_len] for h in range(nh)]
        r_v = [_dot(loc[c, h]["lhs_v"], v_new[h].astype(BF16)) for h in range(nh)]
        for h in range(nh):
            cols = slice(h * dk, (h + 1) * dk)
            state[h] = state[h] * loc[c, h]["e_last"] + r_v[h][c_len:]
            o = r_s[h][c_len:] + r_v[h][:c_len]
            o = _rms(o, nw_ref[...]) * z_ref[rows, cols].astype(F32)
            o_ref[rows, cols] = o.astype(o_ref.dtype)
        between()
    assert slot[0] == n_slots
    for h in range(nh):
        s_scr[h] = state[h]


def _gdn_gates(q3, k3, v3, z3, gcol, grow, norm_w, h2d, w_main, gate_col0, n_gate_cols,
               heads, dk, dv, chunk, nh, tb, tm):
    b, t, _ = q3.shape
    m, d = h2d.shape
    assert dk == V7X_LANES and dv == V7X_LANES and heads % nh == 0 and t % tb == 0 and tb % chunk == 0
    w = nh * dk
    n_hg, n_t = heads // nh, t // tb
    steps = b * n_hg * n_t
    assert (m // tm) * n_gate_cols % steps == 0
    tn = (m // tm) * n_gate_cols // steps
    assert tn % min(tn, V7X_MXU_COLS) == 0 and n_gate_cols % tn == 0 and gate_col0 % tn == 0
    n_j = n_gate_cols // tn
    jb = gate_col0 // tn
    kern = functools.partial(_gdn_gates_kernel, heads=heads, nh=nh, chunk=chunk, dk=dk, n_hg=n_hg, n_t=n_t)

    def gdn_idx(s):
        return s // (n_hg * n_t), s % n_t, (s // n_t) % n_hg

    col = pl.BlockSpec((None, tb, w), lambda s: gdn_idx(s))
    return pl.pallas_call(
        kern,
        grid=(steps,),
        in_specs=[col, col, col, col,
                  pl.BlockSpec((None, tb, V7X_LANES), lambda s: (gdn_idx(s)[0], gdn_idx(s)[1], 0)),
                  pl.BlockSpec((None, tb // chunk, 2 * V7X_SUBLANES, chunk),
                               lambda s: (gdn_idx(s)[0], gdn_idx(s)[1], 0, 0)),
                  pl.BlockSpec((1, dv), lambda s: (0, 0)),
                  pl.BlockSpec((tm, d), lambda s: (s // n_j, 0)),
                  pl.BlockSpec((d, tn), lambda s: (0, jb + s % n_j))],
        out_specs=[col, pl.BlockSpec((tm, tn), lambda s: (s // n_j, s % n_j))],
        out_shape=[jax.ShapeDtypeStruct((b, t, heads * dv), BF16),
                   jax.ShapeDtypeStruct((m, n_gate_cols), BF16)],
        scratch_shapes=[pltpu.VMEM((nh, dk, dv), F32)],
        compiler_params=_params("arbitrary"),
        name="gdn_gates",
    )(q3, k3, v3, z3, gcol, grow, norm_w, h2d, w_main)


def _merge_kernel(ya_ref, yb_ref, wl_ref, wg_ref, gl_ref, gg_ref, o_ref):
    pa = _dot(ya_ref[...], wl_ref[...])
    pb = _dot(yb_ref[...], wg_ref[...])
    o_ref[...] = (gl_ref[...].astype(F32) * pa + gg_ref[...].astype(F32) * pb).astype(o_ref.dtype)


def _merge(ya2, yb2, w_l, w_g, gates, tm, tn):
    m, ka = ya2.shape
    kb = yb2.shape[1]
    d = w_l.shape[1]
    assert d % tn == 0
    gg0 = d // tn
    return pl.pallas_call(
        _merge_kernel,
        grid=(m // tm, d // tn),
        in_specs=[pl.BlockSpec((tm, ka), lambda i, j: (i, 0)),
                  pl.BlockSpec((tm, kb), lambda i, j: (i, 0)),
                  pl.BlockSpec((ka, tn), lambda i, j: (0, j)),
                  pl.BlockSpec((kb, tn), lambda i, j: (0, j)),
                  pl.BlockSpec((tm, tn), lambda i, j: (i, j)),
                  pl.BlockSpec((tm, tn), lambda i, j: (i, gg0 + j))],
        out_specs=pl.BlockSpec((tm, tn), lambda i, j: (i, j)),
        out_shape=jax.ShapeDtypeStruct((m, d), BF16),
        compiler_params=_params("arbitrary", "arbitrary"),
        name="merge",
    )(ya2, yb2, w_l, w_g, gates, gates)


def _mm_kernel(*refs, relu2, cast):
    if cast:
        a_ref, w_ref, ci_ref, o_ref, co_ref = refs
        co_ref[...] = ci_ref[...].astype(co_ref.dtype)
    else:
        a_ref, w_ref, o_ref = refs
    y = _dot(a_ref[...], w_ref[...])
    if relu2:
        y = jnp.square(jnp.maximum(y, 0.0))
    o_ref[...] = y.astype(o_ref.dtype)


def _mm(a, w, out_dtype, tm, tn, relu2=False, cast_src=None, name="mm"):
    m, k = a.shape
    n = w.shape[1]
    grid = (m // tm, n // tn)
    in_specs = [pl.BlockSpec((tm, k), lambda i, j: (i, 0)),
                pl.BlockSpec((k, tn), lambda i, j: (0, j))]
    out_specs = pl.BlockSpec((tm, tn), lambda i, j: (i, j))
    out_shape = jax.ShapeDtypeStruct((m, n), out_dtype)
    args = (a, w)
    if cast_src is not None:
        cast_spec, cast_shape = _cast_job(cast_src, grid)
        in_specs, args = in_specs + [cast_spec], args + (cast_src,)
        out_specs, out_shape = [out_specs, cast_spec], [out_shape, cast_shape]
    return pl.pallas_call(
        functools.partial(_mm_kernel, relu2=relu2, cast=cast_src is not None),
        grid=grid,
        in_specs=in_specs,
        out_specs=out_specs,
        out_shape=out_shape,
        compiler_params=_params("arbitrary", "arbitrary"),
        name=name,
    )(*args)


def _mm_acc_kernel(a_ref, w_ref, o_ref, acc_ref):
    kk = pl.program_id(2)

    @pl.when(kk == 0)
    def _():
        acc_ref[...] = jnp.zeros_like(acc_ref)

    acc_ref[...] += _dot(a_ref[...], w_ref[...])

    @pl.when(kk == pl.num_programs(2) - 1)
    def _():
        o_ref[...] = acc_ref[...].astype(o_ref.dtype)


def _mm_acc(a, w, out_dtype, tm, tn, tk, name="mm_acc"):
    m, k = a.shape
    n = w.shape[1]
    return pl.pallas_call(
        _mm_acc_kernel,
        grid=(m // tm, n // tn, k // tk),
        in_specs=[pl.BlockSpec((tm, tk), lambda i, j, kk: (i, kk)),
                  pl.BlockSpec((tk, tn), lambda i, j, kk: (kk, j))],
        out_specs=pl.BlockSpec((tm, tn), lambda i, j, kk: (i, j)),
        out_shape=jax.ShapeDtypeStruct((m, n), out_dtype),
        scratch_shapes=[pltpu.VMEM((tm, tn), F32)],
        compiler_params=_params("arbitrary", "arbitrary", "arbitrary"),
        name=name,
    )(a, w)


def _resid_prenorm_kernel(y_ref, x_ref, mod_ref, pw_ref, nw_ref, x1_ref, h_ref):
    x1 = x_ref[...] + mod_ref[2:3, :] * _rms(y_ref[...].astype(F32), pw_ref[...])
    x1_ref[...] = x1
    h = _rms(x1, nw_ref[...]) * (1.0 + mod_ref[4:5, :]) + mod_ref[3:4, :]
    h_ref[...] = h.astype(h_ref.dtype)


def _resid_prenorm(y, x2, mod3, post_w, pre_w, seq, tm):
    m, d = x2.shape
    row = pl.BlockSpec((tm, d), lambda i: (i, 0))
    vec = pl.BlockSpec((1, d), lambda i: (0, 0))
    return pl.pallas_call(
        _resid_prenorm_kernel,
        grid=(m // tm,),
        in_specs=[row, row, pl.BlockSpec((None, 6, d), lambda i: ((i * tm) // seq, 0, 0)), vec, vec],
        out_specs=[row, row],
        out_shape=[jax.ShapeDtypeStruct((m, d), F32), jax.ShapeDtypeStruct((m, d), BF16)],
        compiler_params=_params("arbitrary"),
        name="resid_prenorm",
    )(y, x2, mod3, post_w, pre_w)


def _resid_kernel(y_ref, x_ref, mod_ref, pw_ref, o_ref):
    o_ref[...] = x_ref[...] + mod_ref[5:6, :] * _rms(y_ref[...].astype(F32), pw_ref[...])


def _resid(y, x1, mod3, post_w, seq, tm):
    m, d = x1.shape
    row = pl.BlockSpec((tm, d), lambda i: (i, 0))
    return pl.pallas_call(
        _resid_kernel,
        grid=(m // tm,),
        in_specs=[row, row, pl.BlockSpec((None, 6, d), lambda i: ((i * tm) // seq, 0, 0)),
                  pl.BlockSpec((1, d), lambda i: (0, 0))],
        out_specs=row,
        out_shape=jax.ShapeDtypeStruct((m, d), F32),
        compiler_params=_params("arbitrary"),
        name="resid",
    )(y, x1, mod3, post_w)


def _tile(n, pref):
    t = min(n, pref)
    while n % t:
        t //= 2
    return t


def _layer(x2, mod3, batch, seq, mix_pre_norm, mix_post_norm, w_in, lru_conv_w, lru_conv_b,
           lru_gate_a_w, lru_gate_a_b, lru_gate_i_w, lru_gate_i_b, lru_lambda, gdn_conv_w,
           gdn_a_log, gdn_dt_bias, gdn_out_norm, w_branch_lru, w_branch_gdn, w_out,
           mlp_pre_norm, mlp_post_norm, w_mlp_up, w_mlp_down):
    m, d = x2.shape
    lw = lru_lambda.shape[-1]
    nblk = lru_gate_a_w.shape[0]
    heads = gdn_a_log.shape[-1]
    dv = gdn_out_norm.shape[-1]
    val = heads * dv
    conv_dim = gdn_conv_w.shape[-1]
    key = (conv_dim - val) // 2
    dk = key // heads
    lanes = V7X_LANES

    o_q = 2 * lw
    o_z = o_q + conv_dim
    o_a = o_z + val
    n_main = o_a + 2 * d
    w_in_t = jnp.swapaxes(w_in, 0, 1)
    w_main = _wprep(w_in_t, n_main, o_a, 2 * heads, _tile(d, 1024), _tile(o_a, 1024))
    w_ab = jnp.pad(w_in[:, o_a:o_a + 2 * heads], ((0, 0), (0, lanes - 2 * heads))).astype(BF16)

    h, ab = _prenorm(x2, mod3, mix_pre_norm[None, :], w_ab, seq, _tile(seq, 512))

    tm = _tile(seq, 1024)
    tn = _tile(min(lw, key, val), 1024)
    proj = functools.partial(_proj, h, w_main, seq=seq, tm=tm, tn=tn)
    xa = proj(0, lw, conv_w=lru_conv_w, conv_b=lru_conv_b[None, :], name="proj_lru_x")
    gel, w_out_bf = proj(lw, lw, act="gelu", cast_src=w_out, name="proj_lru_gate")
    gq = proj(o_q, key, act="silu", conv_w=gdn_conv_w[:, :key], l2_scale=dk ** -0.5, group=dk, name="proj_q")
    gk = proj(o_q + key, key, act="silu", conv_w=gdn_conv_w[:, key:2 * key], l2_scale=1.0, group=dk, name="proj_k")
    gv, w_bg_bf = proj(o_q + 2 * key, val, act="silu", conv_w=gdn_conv_w[:, 2 * key:], cast_src=w_branch_gdn,
                       name="proj_v")
    gz, w_bl_bf = proj(o_z, val, act="silu", cast_src=w_branch_lru, name="proj_z")

    groups = 4 if nblk % 4 == 0 else 1
    w_ai = jnp.concatenate([lru_gate_a_w, lru_gate_i_w], axis=-1).astype(BF16)
    b_ai = jnp.concatenate([lru_gate_a_b, lru_gate_i_b], axis=-1)[:, None, :]
    as3 = lambda a: a.reshape(batch, seq, a.shape[-1])
    ya = _lru(as3(xa), as3(gel), w_ai, b_ai, lru_lambda[None, :], groups, _tile(seq, LRU_ROWS_PER_STEP))

    pad_h = lambda v: jnp.pad(v, (0, lanes - heads))[None, :]
    gcol, grow = _gdn_prep(as3(ab), pad_h(gdn_a_log), pad_h(gdn_dt_bias), heads, GDN_CHUNK)
    nh = GDN_HEADS_PER_STEP if heads % GDN_HEADS_PER_STEP == 0 else 1
    yb, gates = _gdn_gates(as3(gq), as3(gk), as3(gv), as3(gz), gcol, grow, gdn_out_norm[None, :],
                           h, w_main, o_a, 2 * d, heads, dk, dv, GDN_CHUNK, nh,
                           _tile(seq, GDN_ROWS_PER_STEP), tm)

    tm_mid = _tile(m, 512)
    merged = _merge(ya.reshape(m, lw), yb.reshape(m, val), w_bl_bf, w_bg_bf, gates, tm_mid, _tile(d, 1024))
    tm_big = _tile(m, 1024)
    y1, w_up_bf = _mm(merged, w_out_bf, BF16, tm_big, _tile(d, 1024), cast_src=w_mlp_up, name="out_proj")
    x1, h2 = _resid_prenorm(y1, x2, mod3, mix_post_norm[None, :], mlp_pre_norm[None, :], seq, _tile(m, 256))

    d_ff = w_mlp_up.shape[1]
    hid, w_down_bf = _mm(h2, w_up_bf, BF16, tm_big, _tile(d_ff, 1024), relu2=True, cast_src=w_mlp_down,
                         name="mlp_up")
    y2 = _mm_acc(hid, w_down_bf, BF16, tm_big, _tile(d, 1024), _tile(d_ff, 4096), name="mlp_down")
    return _resid(y2, x1, mod3, mlp_post_norm[None, :], seq, _tile(seq, 512))


def kernel(x, c, w_ada, b_ada, mix_pre_norm, mix_post_norm, w_in, lru_conv_w, lru_conv_b, lru_gate_a_w, lru_gate_a_b, lru_gate_i_w, lru_gate_i_b, lru_lambda, gdn_conv_w, gdn_a_log, gdn_dt_bias, gdn_out_norm, w_branch_lru, w_branch_gdn, w_out, mlp_pre_norm, mlp_post_norm, w_mlp_up, w_mlp_down):
    batch, seq, d = x.shape
    depth = w_ada.shape[0]
    x2 = x.reshape(batch * seq, d)
    c_pad = jnp.pad(c, ((0, V7X_SUBLANES - batch % V7X_SUBLANES), (0, 0))) if batch % V7X_SUBLANES else c
    for l in range(depth):
        mod = _adaln(c_pad, w_ada[l], b_ada[l][None, :], _tile(6 * d, 512))
        mod3 = mod[:batch].reshape(batch, 6, d)
        x2 = _layer(x2, mod3, batch, seq, mix_pre_norm[l], mix_post_norm[l], w_in[l], lru_conv_w[l],
                    lru_conv_b[l], lru_gate_a_w[l], lru_gate_a_b[l], lru_gate_i_w[l], lru_gate_i_b[l],
                    lru_lambda[l], gdn_conv_w[l], gdn_a_log[l], gdn_dt_bias[l], gdn_out_norm[l],
                    w_branch_lru[l], w_branch_gdn[l], w_out[l], mlp_pre_norm[l], mlp_post_norm[l],
                    w_mlp_up[l], w_mlp_down[l])
    return x2.reshape(batch, seq, d)
```
